```python
import math
import jax, jax.numpy as jnp
from jax import lax
import numpy as np

D_MODEL = 1024
BATCH = 16
SEQ = 4096
DEPTH = 4

MIX_WIDTH = D_MODEL
A_WIDTH = MIX_WIDTH // 2
A_GROUPS = 8
A_GROUP_DIM = A_WIDTH // A_GROUPS
CHUNK = 128
B_HEADS = 8
HEAD_DIM = 64
B_KV_HEADS = 2
B_GROUP = B_HEADS // B_KV_HEADS
B_WIDTH = B_HEADS * HEAD_DIM
KV_WIDTH = B_KV_HEADS * HEAD_DIM
WINDOW = 128
BLOCK = 128
REL_BUCKETS = 32
REL_MAX_DIST = 128
NORM_EPS = 1e-6
IN_WIDTH = 3 * A_WIDTH + 2 * B_WIDTH + 2 * KV_WIDTH

kernel_name = "hybrid_gmlp_swa_sink_sandwich"


def rms_norm(x, g):
    xf = x.astype(jnp.float32)
    y = xf * lax.rsqrt(jnp.mean(xf * xf, axis=-1, keepdims=True) + NORM_EPS)
    return (y * g.astype(jnp.float32)).astype(x.dtype)


def layer_norm(x, g, b):
    xf = x.astype(jnp.float32)
    mu = jnp.mean(xf, axis=-1, keepdims=True)
    xc = xf - mu
    y = xc * lax.rsqrt(jnp.mean(xc * xc, axis=-1, keepdims=True) + NORM_EPS)
    return (y * g.astype(jnp.float32) + b.astype(jnp.float32)).astype(x.dtype)


def t5_bucket(dist):
    max_exact = REL_BUCKETS // 2
    safe = jnp.maximum(dist, 1).astype(jnp.float32)
    large = max_exact + (jnp.log(safe / max_exact) / math.log(REL_MAX_DIST / max_exact)
                         * (REL_BUCKETS - max_exact)).astype(jnp.int32)
    large = jnp.minimum(large, REL_BUCKETS - 1)
    return jnp.where(dist < max_exact, dist, large)


def chunked_spatial_gating(u, v, ln_g, ln_b, w_s, b_s):
    bsz, seq, _ = v.shape
    n_chunks = seq // CHUNK
    vn = layer_norm(v, ln_g, ln_b).reshape(bsz, n_chunks, CHUNK, A_GROUPS, A_GROUP_DIM)
    causal = jnp.tril(jnp.ones((CHUNK, CHUNK), dtype=bool))
    w = jnp.where(causal[None], w_s, jnp.zeros_like(w_s))
    mixed = jnp.einsum('hij,bcjhd->bcihd', w, vn) + b_s.T[None, None, :, :, None]
    return u * mixed.reshape(bsz, seq, A_WIDTH)


def sliding_window_sink_attention(q, k, v, sinks, rel_bias):
    bsz, seq = q.shape[0], q.shape[1]
    nb = seq // BLOCK
    qb = q.reshape(bsz, nb, BLOCK, B_KV_HEADS, B_GROUP, HEAD_DIM)
    pad = jnp.zeros((bsz, BLOCK, B_KV_HEADS, HEAD_DIM), k.dtype)
    kp = jnp.concatenate([pad, k], axis=1).reshape(bsz, nb + 1, BLOCK, B_KV_HEADS, HEAD_DIM)
    vp = jnp.concatenate([pad, v], axis=1).reshape(bsz, nb + 1, BLOCK, B_KV_HEADS, HEAD_DIM)
    kb = jnp.concatenate([kp[:, :-1], kp[:, 1:]], axis=2)
    vb = jnp.concatenate([vp[:, :-1], vp[:, 1:]], axis=2)

    logits = jnp.einsum('bnqkgd,bnskd->bnkgqs', qb, kb).astype(jnp.float32) * (HEAD_DIM ** -0.5)

    q_loc = jnp.arange(BLOCK)[:, None]
    k_loc = jnp.arange(2 * BLOCK)[None, :]
    dist = q_loc + BLOCK - k_loc
    in_window = (dist >= 0) & (dist < WINDOW)
    key_pos = (jnp.arange(nb)[:, None] - 1) * BLOCK + jnp.arange(2 * BLOCK)[None, :]
    mask = in_window[None] & (key_pos >= 0)[:, None, :]

    bias = rel_bias.astype(jnp.float32)[t5_bucket(jnp.maximum(dist, 0))]
    bias = bias.transpose(2, 0, 1).reshape(B_KV_HEADS, B_GROUP, BLOCK, 2 * BLOCK)
    logits = jnp.where(mask[None, :, None, None], logits + bias, jnp.float32(-1e30))

    sink = sinks.astype(jnp.float32).reshape(B_KV_HEADS, B_GROUP)[None, None, :, :, None, None]
    m = jnp.maximum(jnp.max(logits, axis=-1, keepdims=True), sink)
    e = jnp.exp(logits - m)
    probs = e / (jnp.sum(e, axis=-1, keepdims=True) + jnp.exp(sink - m))
    out = jnp.einsum('bnkgqs,bnskd->bnqkgd', probs.astype(vb.dtype), vb)
    return out.reshape(bsz, seq, B_WIDTH)


def hybrid_layer(x, pre_g, w_in, ln_g, ln_b, w_s, b_s, sinks, rel_bias, w_out, post_g):
    bsz, seq, _ = x.shape
    h = rms_norm(x, pre_g)
    proj = h @ w_in
    splits = [2 * A_WIDTH, 3 * A_WIDTH, 3 * A_WIDTH + B_WIDTH,
              3 * A_WIDTH + B_WIDTH + KV_WIDTH, 3 * A_WIDTH + B_WIDTH + 2 * KV_WIDTH]
    a_uv, a_z, q, k, v, b_z = jnp.split(proj, splits, axis=-1)

    a_uv = jax.nn.gelu(a_uv)
    u, vv = jnp.split(a_uv, 2, axis=-1)
    y_a = chunked_spatial_gating(u, vv, ln_g, ln_b, w_s, b_s) * jax.nn.silu(a_z)

    q = q.reshape(bsz, seq, B_HEADS, HEAD_DIM)
    k = k.reshape(bsz, seq, B_KV_HEADS, HEAD_DIM)
    v = v.reshape(bsz, seq, B_KV_HEADS, HEAD_DIM)
    y_b = sliding_window_sink_attention(q, k, v, sinks, rel_bias) * jax.nn.silu(b_z)

    y = jnp.concatenate([y_a, y_b], axis=-1) @ w_out
    return x + rms_norm(y, post_g)


def setup_inputs(seed: int = 0) -> dict:
    key = jax.random.key(seed)
    ks = jax.random.split(key, 12)
    f32 = jnp.float32
    x = jax.random.normal(ks[0], (BATCH, SEQ, D_MODEL), f32)
    pre_norm_g = 1.0 + 0.05 * jax.random.normal(ks[1], (DEPTH, D_MODEL), f32)
    w_in = jax.random.normal(ks[2], (DEPTH, D_MODEL, IN_WIDTH), f32) * (D_MODEL ** -0.5)
    ln_v_g = 1.0 + 0.05 * jax.random.normal(ks[3], (DEPTH, A_WIDTH), f32)
    ln_v_b = 0.02 * jax.random.normal(ks[4], (DEPTH, A_WIDTH), f32)
    w_spatial = jax.random.normal(ks[5], (DEPTH, A_GROUPS, CHUNK, CHUNK), f32) * (CHUNK ** -0.5)
    b_spatial = 1.0 + 0.1 * jax.random.normal(ks[6], (DEPTH, A_GROUPS, CHUNK), f32)
    sinks = 0.5 * jax.random.normal(ks[7], (DEPTH, B_HEADS), f32)
    rel_bias = 0.5 * jax.random.normal(ks[8], (REL_BUCKETS, B_HEADS), f32)
    w_out = jax.random.normal(ks[9], (DEPTH, MIX_WIDTH, D_MODEL), f32) * (MIX_WIDTH ** -0.5)
    post_norm_g = 1.0 + 0.05 * jax.random.normal(ks[10], (DEPTH, D_MODEL), f32)
    return {"x": x, "pre_norm_g": pre_norm_g, "w_in": w_in, "ln_v_g": ln_v_g, "ln_v_b": ln_v_b,
            "w_spatial": w_spatial, "b_spatial": b_spatial, "sinks": sinks, "rel_bias": rel_bias,
            "w_out": w_out, "post_norm_g": post_norm_g}


def reference(x, pre_norm_g, w_in, ln_v_g, ln_v_b, w_spatial, b_spatial, sinks, rel_bias,
              w_out, post_norm_g):
    for layer in range(DEPTH):
        x = hybrid_layer(x, pre_norm_g[layer], w_in[layer], ln_v_g[layer], ln_v_b[layer],
                         w_spatial[layer], b_spatial[layer], sinks[layer], rel_bias,
                         w_out[layer], post_norm_g[layer])
    return x
```

```python
import functools

import jax
import jax.numpy as jnp
from jax import lax
from jax.experimental import pallas as pl
from jax.experimental.pallas import tpu as pltpu

D_MODEL = 1024
A_WIDTH = 512
A_GROUPS = 8
A_GROUP_DIM = A_WIDTH // A_GROUPS
B_HEADS = 8
HEAD_DIM = 64
B_KV_HEADS = 2
B_GROUP = B_HEADS // B_KV_HEADS
B_WIDTH = B_HEADS * HEAD_DIM
KV_WIDTH = B_KV_HEADS * HEAD_DIM
BLOCK = 128
REL_BUCKETS = 32
REL_MAX_DIST = 128
NORM_EPS = 1e-6
IN_WIDTH = 3 * A_WIDTH + 2 * B_WIDTH + 2 * KV_WIDTH
NEG_BIG = -1e30

UV_LO, AZ_LO, Q_LO, K_LO, V_LO, BZ_LO = 0, 2 * A_WIDTH, 3 * A_WIDTH, 3 * A_WIDTH + B_WIDTH, \
    3 * A_WIDTH + B_WIDTH + KV_WIDTH, 3 * A_WIDTH + B_WIDTH + 2 * KV_WIDTH

V7X_LANES = 128
N_PAIRS = A_WIDTH // V7X_LANES
BLOCKS_PER_STEP = 4
TOKENS_PER_STEP = BLOCKS_PER_STEP * BLOCK
VMEM_LIMIT_BYTES = 48 * 1024 * 1024


def _head_pair_permutation():
    cols = []
    for g in range(B_GROUP):
        for kv in range(B_KV_HEADS):
            head = kv * B_GROUP + g
            cols.extend(range(head * HEAD_DIM, (head + 1) * HEAD_DIM))
    return jnp.asarray(cols, dtype=jnp.int32)


def _combined_bucket_map():
    i = jnp.arange(BLOCK)[:, None]
    j = jnp.arange(BLOCK)[None, :]
    dist = jnp.where(j <= i, i - j, i + BLOCK - j)
    max_exact = REL_BUCKETS // 2
    safe = jnp.maximum(dist, 1).astype(jnp.float32)
    import math
    large = max_exact + (jnp.log(safe / max_exact) / math.log(REL_MAX_DIST / max_exact)
                         * (REL_BUCKETS - max_exact)).astype(jnp.int32)
    large = jnp.minimum(large, REL_BUCKETS - 1)
    return jnp.where(dist < max_exact, dist, large).astype(jnp.int32)


def _layer_kernel(relb_ref, sink_ref,
                  x_ref, preg_ref, win_ref, lng_ref, lnb_ref, wsp_ref, bsx_ref, bucket_ref,
                  wout_ref, postg_ref,
                  o_ref,
                  kz0_ref, kz1_ref, va0_ref, va1_ref, wcat_ref, bias_ref):
    f32, bf16 = jnp.float32, jnp.bfloat16
    t = pl.program_id(1)
    first_step = (pl.program_id(0) == 0) & (t == 0)

    row = lax.broadcasted_iota(jnp.int32, (BLOCK, BLOCK), 0)
    col = lax.broadcasted_iota(jnp.int32, (BLOCK, BLOCK), 1)
    tril = col <= row
    low_half = col < HEAD_DIM

    @pl.when(first_step)
    def _init():
        wcol = lax.broadcasted_iota(jnp.int32, (BLOCK, 2 * BLOCK), 1)
        wrow = lax.broadcasted_iota(jnp.int32, (BLOCK, 2 * BLOCK), 0)
        wmask = (wcol % BLOCK) <= wrow
        for p in range(N_PAIRS):
            wcat_ref[p] = jnp.where(wmask, wsp_ref[p], 0.0).astype(bf16)
        bucket = bucket_ref[...]
        for h in range(B_HEADS):
            bias_ref[0, h] = jnp.zeros((BLOCK, BLOCK), f32)

        def fill(b, carry):
            hit = bucket == b
            for h in range(B_HEADS):
                bias_ref[0, h] = jnp.where(hit, relb_ref[b, h], bias_ref[0, h])
            return carry
        lax.fori_loop(0, REL_BUCKETS, fill, 0)
        for h in range(B_HEADS):
            bias_ref[1, h] = jnp.where(tril, bias_ref[0, h], NEG_BIG)

    carry_refs = (kz0_ref, kz1_ref, va0_ref, va1_ref)

    @pl.when(t == 0)
    def _reset_carry():
        for ref in carry_refs:
            ref[...] = jnp.zeros((BLOCK, BLOCK), bf16)

    first_flag = (t == 0).astype(jnp.int32)
    prev = tuple(ref[...] for ref in carry_refs)
    ones_cols = jnp.ones((2 * BLOCK, BLOCK), bf16)

    for blk in range(BLOCKS_PER_STEP):
        r0 = blk * BLOCK
        xb = x_ref[0, r0:r0 + BLOCK, :]
        ms = jnp.mean(xb * xb, axis=-1, keepdims=True)
        h = (xb * lax.rsqrt(ms + NORM_EPS) * preg_ref[...]).astype(bf16)

        def proj(lo, width):
            return jnp.dot(h, win_ref[:, lo:lo + width], preferred_element_type=f32)

        uv = jax.nn.gelu(proj(UV_LO, 2 * A_WIDTH))
        u, v = uv[:, :A_WIDTH], uv[:, A_WIDTH:]
        mu = jnp.mean(v, axis=-1, keepdims=True)
        vc = v - mu
        var = jnp.mean(vc * vc, axis=-1, keepdims=True)
        vn = vc * lax.rsqrt(var + NORM_EPS) * lng_ref[...] + lnb_ref[...]
        mixed = []
        for p in range(N_PAIRS):
            slab = vn[:, p * BLOCK:(p + 1) * BLOCK]
            rhs = jnp.concatenate([jnp.where(low_half, slab, 0.0).astype(bf16),
                                   jnp.where(low_half, 0.0, slab).astype(bf16)], axis=0)
            mixed.append(jnp.dot(wcat_ref[p], rhs, preferred_element_type=f32))
        mixed = jnp.concatenate(mixed, axis=1) + bsx_ref[...]
        y_a = (u * mixed) * jax.nn.silu(proj(AZ_LO, A_WIDTH))

        q = (proj(Q_LO, B_WIDTH) * (HEAD_DIM ** -0.5)).astype(bf16)
        k = proj(K_LO, KV_WIDTH)
        vv = proj(V_LO, KV_WIDTH)
        cur = (jnp.where(low_half, k, 0.0).astype(bf16), jnp.where(low_half, 0.0, k).astype(bf16),
               jnp.where(low_half, vv, 1.0).astype(bf16), jnp.where(low_half, 1.0, vv).astype(bf16))

        q_stack = jnp.concatenate([q[:, g * BLOCK:(g + 1) * BLOCK] for g in range(B_GROUP)], axis=0)
        flag = first_flag if blk == 0 else 0
        num, den = [], []
        for kv in range(B_KV_HEADS):
            keys = jnp.concatenate([prev[kv], cur[kv]], axis=0)
            vals = jnp.concatenate([jnp.concatenate([prev[2 + kv], cur[2 + kv]], axis=0), ones_cols],
                                   axis=1)
            s = lax.dot_general(q_stack, keys, (((1,), (1,)), ((), ())),
                                preferred_element_type=f32)
            probs, sink_terms = [], []
            for g in range(B_GROUP):
                head = kv * B_GROUP + g
                sg = s[g * BLOCK:(g + 1) * BLOCK, :]
                logits = jnp.where(tril, sg[:, BLOCK:], sg[:, :BLOCK]) + bias_ref[flag, head]
                sink = sink_ref[head]
                m = jnp.maximum(jnp.max(logits, axis=-1, keepdims=True), sink)
                e = jnp.exp(logits - m)
                e_cur = jnp.where(tril, e, 0.0)
                probs.append(jnp.concatenate([e - e_cur, e_cur], axis=1).astype(bf16))
                sink_terms.append(jnp.exp(sink - m))
            pv = jnp.dot(jnp.concatenate(probs, axis=0), vals,
                         preferred_element_type=f32)
            num.append([pv[g * BLOCK:(g + 1) * BLOCK, :BLOCK] for g in range(B_GROUP)])
            den.append([pv[g * BLOCK:(g + 1) * BLOCK, BLOCK:] + sink_terms[g] for g in range(B_GROUP)])
        attn = jnp.concatenate(
            [jnp.where(low_half, num[0][g], num[1][g]) / jnp.where(low_half, den[0][g], den[1][g])
             for g in range(B_GROUP)], axis=1)
        y_b = attn * jax.nn.silu(proj(BZ_LO, B_WIDTH))

        y = jnp.concatenate([y_a, y_b], axis=1).astype(bf16)
        o = jnp.dot(y, wout_ref[...], preferred_element_type=f32)
        os_ = jnp.mean(o * o, axis=-1, keepdims=True)
        o_ref[0, r0:r0 + BLOCK, :] = xb + o * lax.rsqrt(os_ + NORM_EPS) * postg_ref[...]
        prev = cur

    for ref, val in zip(carry_refs, prev):
        ref[...] = val


def _layer(x, rel_bias, sinks, pre_g, w_in, ln_g, ln_b, w_sp, bs_x, bucket, w_out, post_g):
    bsz, seq, _ = x.shape
    steps = seq // TOKENS_PER_STEP
    const2 = lambda b, t: (0, 0)
    const3 = lambda b, t: (0, 0, 0)
    smem = pl.BlockSpec(memory_space=pltpu.SMEM)
    return pl.pallas_call(
        _layer_kernel,
        grid=(bsz, steps),
        in_specs=[
            smem, smem,
            pl.BlockSpec((1, TOKENS_PER_STEP, D_MODEL), lambda b, t: (b, t, 0)),
            pl.BlockSpec((1, D_MODEL), const2),
            pl.BlockSpec((D_MODEL, IN_WIDTH), const2),
            pl.BlockSpec((1, A_WIDTH), const2),
            pl.BlockSpec((1, A_WIDTH), const2),
            pl.BlockSpec((N_PAIRS, BLOCK, 2 * BLOCK), const3),
            pl.BlockSpec((BLOCK, A_WIDTH), const2),
            pl.BlockSpec((BLOCK, BLOCK), const2),
            pl.BlockSpec((D_MODEL, D_MODEL), const2),
            pl.BlockSpec((1, D_MODEL), const2),
        ],
        out_specs=pl.BlockSpec((1, TOKENS_PER_STEP, D_MODEL), lambda b, t: (b, t, 0)),
        out_shape=jax.ShapeDtypeStruct(x.shape, x.dtype),
        scratch_shapes=[
            pltpu.VMEM((BLOCK, BLOCK), jnp.bfloat16),
            pltpu.VMEM((BLOCK, BLOCK), jnp.bfloat16),
            pltpu.VMEM((BLOCK, BLOCK), jnp.bfloat16),
            pltpu.VMEM((BLOCK, BLOCK), jnp.bfloat16),
            pltpu.VMEM((N_PAIRS, BLOCK, 2 * BLOCK), jnp.bfloat16),
            pltpu.VMEM((2, B_HEADS, BLOCK, BLOCK), jnp.float32),
        ],
        compiler_params=pltpu.CompilerParams(
            dimension_semantics=("arbitrary", "arbitrary"),
            vmem_limit_bytes=VMEM_LIMIT_BYTES),
        name="hybrid_layer",
    )(rel_bias, sinks, x, pre_g, w_in, ln_g, ln_b, w_sp, bs_x, bucket, w_out, post_g)


@jax.jit
def kernel(x, pre_norm_g, w_in, ln_v_g, ln_v_b, w_spatial, b_spatial, sinks, rel_bias, w_out, post_norm_g):
    depth = w_in.shape[0]
    perm = _head_pair_permutation()
    col_perm = jnp.concatenate([jnp.arange(Q_LO), Q_LO + perm, jnp.arange(K_LO, BZ_LO), BZ_LO + perm])
    row_perm = jnp.concatenate([jnp.arange(A_WIDTH), A_WIDTH + perm])
    w_in_p = w_in[:, :, col_perm].astype(jnp.bfloat16)
    w_out_p = w_out[:, row_perm, :].astype(jnp.bfloat16)
    w_sp = w_spatial.reshape(depth, N_PAIRS, 2, BLOCK, BLOCK).transpose(0, 1, 3, 2, 4)
    w_sp = w_sp.reshape(depth, N_PAIRS, BLOCK, 2 * BLOCK)
    bs_x = jnp.repeat(jnp.swapaxes(b_spatial, 1, 2), A_GROUP_DIM, axis=2)
    bucket = _combined_bucket_map()
    for layer in range(depth):
        x = _layer(x, rel_bias, sinks[layer], pre_norm_g[layer][None], w_in_p[layer],
                   ln_v_g[layer][None], ln_v_b[layer][None], w_sp[layer], bs_x[layer], bucket,
                   w_out_p[layer], post_norm_g[layer][None])
    return x
```

```python
import math

import jax
import jax.numpy as jnp
from jax import lax
from jax.experimental import pallas as pl
from jax.experimental.pallas import tpu as pltpu

D_MODEL = 1024
A_WIDTH = 512
A_GROUPS = 8
A_GROUP_DIM = A_WIDTH // A_GROUPS
B_HEADS = 8
HEAD_DIM = 64
B_KV_HEADS = 2
B_GROUP = B_HEADS // B_KV_HEADS
B_WIDTH = B_HEADS * HEAD_DIM
KV_WIDTH = B_KV_HEADS * HEAD_DIM
BLOCK = 128
REL_BUCKETS = 32
REL_MAX_DIST = 128
NORM_EPS = 1e-6
IN_WIDTH = 3 * A_WIDTH + 2 * B_WIDTH + 2 * KV_WIDTH
NEG_BIG = -1e30

UV_LO, AZ_LO, Q_LO, K_LO, V_LO, BZ_LO = 0, 2 * A_WIDTH, 3 * A_WIDTH, 3 * A_WIDTH + B_WIDTH, \
    3 * A_WIDTH + B_WIDTH + KV_WIDTH, 3 * A_WIDTH + B_WIDTH + 2 * KV_WIDTH

V7X_LANES = 128
N_PAIRS = A_WIDTH // V7X_LANES
BLOCKS_PER_STEP = 4
TOKENS_PER_STEP = BLOCKS_PER_STEP * BLOCK
VMEM_LIMIT_BYTES = 48 * 1024 * 1024


def _head_pair_permutation():
    cols = []
    for g in range(B_GROUP):
        for kv in range(B_KV_HEADS):
            head = kv * B_GROUP + g
            cols.extend(range(head * HEAD_DIM, (head + 1) * HEAD_DIM))
    return jnp.asarray(cols, dtype=jnp.int32)


def _combined_bucket_map():
    i = jnp.arange(BLOCK)[:, None]
    j = jnp.arange(BLOCK)[None, :]
    dist = jnp.where(j <= i, i - j, i + BLOCK - j)
    max_exact = REL_BUCKETS // 2
    safe = jnp.maximum(dist, 1).astype(jnp.float32)
    large = max_exact + (jnp.log(safe / max_exact) / math.log(REL_MAX_DIST / max_exact)
                         * (REL_BUCKETS - max_exact)).astype(jnp.int32)
    large = jnp.minimum(large, REL_BUCKETS - 1)
    return jnp.where(dist < max_exact, dist, large).astype(jnp.int32)


def _layer_kernel(relb_ref, sink_ref,
                  x_ref, preg_ref, win_ref, lng_ref, lnb_ref, wsp_ref, bsx_ref, bucket_ref,
                  wout_ref, postg_ref,
                  o_ref,
                  kz0_ref, kz1_ref, va0_ref, va1_ref, wcat_ref, bias_ref):
    f32, bf16 = jnp.float32, jnp.bfloat16
    t = pl.program_id(1)
    first_step = (pl.program_id(0) == 0) & (t == 0)

    row = lax.broadcasted_iota(jnp.int32, (BLOCK, BLOCK), 0)
    col = lax.broadcasted_iota(jnp.int32, (BLOCK, BLOCK), 1)
    tril = col <= row
    low_half = col < HEAD_DIM

    @pl.when(first_step)
    def _init():
        wcol = lax.broadcasted_iota(jnp.int32, (BLOCK, 2 * BLOCK), 1)
        wrow = lax.broadcasted_iota(jnp.int32, (BLOCK, 2 * BLOCK), 0)
        wmask = (wcol % BLOCK) <= wrow
        for p in range(N_PAIRS):
            wcat_ref[p] = jnp.where(wmask, wsp_ref[p], 0.0).astype(bf16)
        bucket = bucket_ref[...]
        for h in range(B_HEADS):
            bias_ref[0, h] = jnp.zeros((BLOCK, BLOCK), f32)

        def fill(b, carry):
            hit = bucket == b
            for h in range(B_HEADS):
                bias_ref[0, h] = jnp.where(hit, relb_ref[b, h], bias_ref[0, h])
            return carry
        lax.fori_loop(0, REL_BUCKETS, fill, 0)
        for h in range(B_HEADS):
            bias_ref[1, h] = jnp.where(tril, bias_ref[0, h], NEG_BIG)

    carry_refs = (kz0_ref, kz1_ref, va0_ref, va1_ref)

    @pl.when(t == 0)
    def _reset_carry():
        for ref in carry_refs:
            ref[...] = jnp.zeros((BLOCK, BLOCK), bf16)

    first_flag = (t == 0).astype(jnp.int32)
    prev = tuple(ref[...] for ref in carry_refs)
    ones_cols = jnp.ones((2 * BLOCK, BLOCK), bf16)

    def in_norm_and_uv(blk):
        r0 = blk * BLOCK
        xb = x_ref[0, r0:r0 + BLOCK, :]
        ms = jnp.mean(xb * xb, axis=-1, keepdims=True)
        h = (xb * lax.rsqrt(ms + NORM_EPS) * preg_ref[...]).astype(bf16)
        uv = jnp.dot(h, win_ref[:, UV_LO:UV_LO + 2 * A_WIDTH], preferred_element_type=f32)
        return {"h": h, "uv": uv}

    def in_gates_and_qkv(st):
        h = st.pop("h")

        def proj(lo, width):
            return jnp.dot(h, win_ref[:, lo:lo + width], preferred_element_type=f32)

        st["q"] = (proj(Q_LO, B_WIDTH) * (HEAD_DIM ** -0.5)).astype(bf16)
        k = proj(K_LO, KV_WIDTH)
        vv = proj(V_LO, KV_WIDTH)
        st["kv"] = (jnp.where(low_half, k, 0.0).astype(bf16), jnp.where(low_half, 0.0, k).astype(bf16),
                    jnp.where(low_half, vv, 1.0).astype(bf16), jnp.where(low_half, 1.0, vv).astype(bf16))
        st["gate_a"] = jax.nn.silu(proj(AZ_LO, A_WIDTH))
        st["gate_b"] = jax.nn.silu(proj(BZ_LO, B_WIDTH))

    def gelu_and_layer_norm(st):
        uv = jax.nn.gelu(st.pop("uv"))
        st["u"], v = uv[:, :A_WIDTH], uv[:, A_WIDTH:]
        mu = jnp.mean(v, axis=-1, keepdims=True)
        vc = v - mu
        var = jnp.mean(vc * vc, axis=-1, keepdims=True)
        vn = vc * lax.rsqrt(var + NORM_EPS) * lng_ref[...] + lnb_ref[...]
        st["vn_rhs"] = [
            jnp.concatenate([jnp.where(low_half, vn[:, p * BLOCK:(p + 1) * BLOCK], 0.0).astype(bf16),
                             jnp.where(low_half, 0.0, vn[:, p * BLOCK:(p + 1) * BLOCK]).astype(bf16)],
                            axis=0) for p in range(N_PAIRS)]

    def scores_and_softmax(st, prev, flag):
        q = st.pop("q")
        q_stack = jnp.concatenate([q[:, g * BLOCK:(g + 1) * BLOCK] for g in range(B_GROUP)], axis=0)
        st["probs"], st["sink_terms"] = [], []
        for kv in range(B_KV_HEADS):
            keys = jnp.concatenate([prev[kv], st["kv"][kv]], axis=0)
            s = lax.dot_general(q_stack, keys, (((1,), (1,)), ((), ())),
                                preferred_element_type=f32)
            probs, sink_terms = [], []
            for g in range(B_GROUP):
                head = kv * B_GROUP + g
                sg = s[g * BLOCK:(g + 1) * BLOCK, :]
                logits = jnp.where(tril, sg[:, BLOCK:], sg[:, :BLOCK]) + bias_ref[flag, head]
                sink = sink_ref[head]
                m = jnp.maximum(jnp.max(logits, axis=-1, keepdims=True), sink)
                e = jnp.exp(logits - m)
                e_cur = jnp.where(tril, e, 0.0)
                probs.append(jnp.concatenate([e - e_cur, e_cur], axis=1).astype(bf16))
                sink_terms.append(jnp.exp(sink - m))
            st["probs"].append(jnp.concatenate(probs, axis=0))
            st["sink_terms"].append(sink_terms)

    def mix_and_attend(st, prev):
        mixed = [jnp.dot(wcat_ref[p], rhs, preferred_element_type=f32)
                 for p, rhs in enumerate(st.pop("vn_rhs"))]
        mixed = jnp.concatenate(mixed, axis=1) + bsx_ref[...]
        y_a = (st.pop("u") * mixed) * st.pop("gate_a")
        num, den = [], []
        for kv in range(B_KV_HEADS):
            vals = jnp.concatenate([jnp.concatenate([prev[2 + kv], st["kv"][2 + kv]], axis=0), ones_cols],
                                   axis=1)
            pv = jnp.dot(st["probs"][kv], vals, preferred_element_type=f32)
            num.append([pv[g * BLOCK:(g + 1) * BLOCK, :BLOCK] for g in range(B_GROUP)])
            den.append([pv[g * BLOCK:(g + 1) * BLOCK, BLOCK:] + st["sink_terms"][kv][g]
                        for g in range(B_GROUP)])
        attn = jnp.concatenate(
            [jnp.where(low_half, num[0][g], num[1][g]) / jnp.where(low_half, den[0][g], den[1][g])
             for g in range(B_GROUP)], axis=1)
        y_b = attn * st.pop("gate_b")
        st["y"] = jnp.concatenate([y_a, y_b], axis=1).astype(bf16)

    def out_norm_residual(st, blk):
        r0 = blk * BLOCK
        o = jnp.dot(st.pop("y"), wout_ref[...], preferred_element_type=f32)
        os_ = jnp.mean(o * o, axis=-1, keepdims=True)
        o_ref[0, r0:r0 + BLOCK, :] = (x_ref[0, r0:r0 + BLOCK, :]
                                      + o * lax.rsqrt(os_ + NORM_EPS) * postg_ref[...])

    pending = None
    for blk in range(BLOCKS_PER_STEP + 1):
        st = in_norm_and_uv(blk) if blk < BLOCKS_PER_STEP else None
        if pending is not None:
            scores_and_softmax(pending, prev, first_flag if blk == 1 else 0)
        if st is not None:
            in_gates_and_qkv(st)
            gelu_and_layer_norm(st)
        if pending is not None:
            mix_and_attend(pending, prev)
            out_norm_residual(pending, blk - 1)
            prev = pending["kv"]
        pending = st

    for ref, val in zip(carry_refs, prev):
        ref[...] = val


def _layer(x, rel_bias, sinks, pre_g, w_in, ln_g, ln_b, w_sp, bs_x, bucket, w_out, post_g):
    bsz, seq, _ = x.shape
    steps = seq // TOKENS_PER_STEP
    const2 = lambda b, t: (0, 0)
    const3 = lambda b, t: (0, 0, 0)
    smem = pl.BlockSpec(memory_space=pltpu.SMEM)
    return pl.pallas_call(
        _layer_kernel,
        grid=(bsz, steps),
        in_specs=[
            smem, smem,
            pl.BlockSpec((1, TOKENS_PER_STEP, D_MODEL), lambda b, t: (b, t, 0)),
            pl.BlockSpec((1, D_MODEL), const2),
            pl.BlockSpec((D_MODEL, IN_WIDTH), const2),
            pl.BlockSpec((1, A_WIDTH), const2),
            pl.BlockSpec((1, A_WIDTH), const2),
            pl.BlockSpec((N_PAIRS, BLOCK, 2 * BLOCK), const3),
            pl.BlockSpec((BLOCK, A_WIDTH), const2),
            pl.BlockSpec((BLOCK, BLOCK), const2),
            pl.BlockSpec((D_MODEL, D_MODEL), const2),
            pl.BlockSpec((1, D_MODEL), const2),
        ],
        out_specs=pl.BlockSpec((1, TOKENS_PER_STEP, D_MODEL), lambda b, t: (b, t, 0)),
        out_shape=jax.ShapeDtypeStruct(x.shape, x.dtype),
        scratch_shapes=[
            pltpu.VMEM((BLOCK, BLOCK), jnp.bfloat16),
            pltpu.VMEM((BLOCK, BLOCK), jnp.bfloat16),
            pltpu.VMEM((BLOCK, BLOCK), jnp.bfloat16),
            pltpu.VMEM((BLOCK, BLOCK), jnp.bfloat16),
            pltpu.VMEM((N_PAIRS, BLOCK, 2 * BLOCK), jnp.bfloat16),
            pltpu.VMEM((2, B_HEADS, BLOCK, BLOCK), jnp.float32),
        ],
        compiler_params=pltpu.CompilerParams(
            dimension_semantics=("arbitrary", "arbitrary"),
            vmem_limit_bytes=VMEM_LIMIT_BYTES),
        name="hybrid_layer",
    )(rel_bias, sinks, x, pre_g, w_in, ln_g, ln_b, w_sp, bs_x, bucket, w_out, post_g)


@jax.jit
def kernel(x, pre_norm_g, w_in, ln_v_g, ln_v_b, w_spatial, b_spatial, sinks, rel_bias, w_out, post_norm_g):
    depth = w_in.shape[0]
    perm = _head_pair_permutation()
    col_perm = jnp.concatenate([jnp.arange(Q_LO), Q_LO + perm, jnp.arange(K_LO, BZ_LO), BZ_LO + perm])
    row_perm = jnp.concatenate([jnp.arange(A_WIDTH), A_WIDTH + perm])
    w_in_p = w_in[:, :, col_perm].astype(jnp.bfloat16)
    w_out_p = w_out[:, row_perm, :].astype(jnp.bfloat16)
    w_sp = w_spatial.reshape(depth, N_PAIRS, 2, BLOCK, BLOCK).transpose(0, 1, 3, 2, 4)
    w_sp = w_sp.reshape(depth, N_PAIRS, BLOCK, 2 * BLOCK)
    bs_x = jnp.repeat(jnp.swapaxes(b_spatial, 1, 2), A_GROUP_DIM, axis=2)
    bucket = _combined_bucket_map()
    for layer in range(depth):
        x = _layer(x, rel_bias, sinks[layer], pre_norm_g[layer][None], w_in_p[layer],
                   ln_v_g[layer][None], ln_v_b[layer][None], w_sp[layer], bs_x[layer], bucket,
                   w_out_p[layer], post_norm_g[layer][None])
    return x
```

```python
import math

import jax
import jax.numpy as jnp
from jax import lax
from jax.experimental import pallas as pl
from jax.experimental.pallas import tpu as pltpu

D_MODEL = 1024
A_WIDTH = 512
A_GROUPS = 8
A_GROUP_DIM = A_WIDTH // A_GROUPS
B_HEADS = 8
HEAD_DIM = 64
B_KV_HEADS = 2
B_GROUP = B_HEADS // B_KV_HEADS
B_WIDTH = B_HEADS * HEAD_DIM
KV_WIDTH = B_KV_HEADS * HEAD_DIM
BLOCK = 128
REL_BUCKETS = 32
REL_MAX_DIST = 128
NORM_EPS = 1e-6
IN_WIDTH = 3 * A_WIDTH + 2 * B_WIDTH + 2 * KV_WIDTH
NEG_BIG = -1e30

UV_LO, AZ_LO, Q_LO, K_LO, V_LO, BZ_LO = 0, 2 * A_WIDTH, 3 * A_WIDTH, 3 * A_WIDTH + B_WIDTH, \
    3 * A_WIDTH + B_WIDTH + KV_WIDTH, 3 * A_WIDTH + B_WIDTH + 2 * KV_WIDTH

V7X_LANES = 128
V7X_MXU_COLS = 256
N_PAIRS = A_WIDTH // V7X_LANES
W_OUT_PITCH = D_MODEL + V7X_LANES
BLOCKS_PER_STEP = 8
TOKENS_PER_STEP = BLOCKS_PER_STEP * BLOCK


def _vmem_limit_bytes():
    f32_bytes, bf16_bytes, buffers = 4, 2, 2
    windows = buffers * (
        2 * TOKENS_PER_STEP * D_MODEL * f32_bytes
        + BLOCK * D_MODEL * f32_bytes
        + D_MODEL * IN_WIDTH * bf16_bytes + D_MODEL * W_OUT_PITCH * bf16_bytes
        + (BLOCK * A_WIDTH + BLOCK * BLOCK) * f32_bytes + N_PAIRS * BLOCK * 2 * BLOCK * f32_bytes)
    scratch = (2 * B_HEADS * BLOCK * BLOCK * f32_bytes
               + 3 * BLOCK * A_WIDTH * f32_bytes
               + (N_PAIRS * 2 + B_WIDTH // BLOCK + 4 + 4 + D_MODEL // BLOCK + N_PAIRS * 2)
               * BLOCK * BLOCK * bf16_bytes)
    temporaries = 2 * BLOCK * IN_WIDTH * f32_bytes * BLOCKS_PER_STEP // 2
    return windows + scratch + temporaries


def _combined_bucket_map():
    i = jnp.arange(BLOCK)[:, None]
    j = jnp.arange(BLOCK)[None, :]
    dist = jnp.where(j <= i, i - j, i + BLOCK - j)
    max_exact = REL_BUCKETS // 2
    safe = jnp.maximum(dist, 1).astype(jnp.float32)
    large = max_exact + (jnp.log(safe / max_exact) / math.log(REL_MAX_DIST / max_exact)
                         * (REL_BUCKETS - max_exact)).astype(jnp.int32)
    large = jnp.minimum(large, REL_BUCKETS - 1)
    return jnp.where(dist < max_exact, dist, large).astype(jnp.int32)


def _gelu_tanh(x):
    k1 = -2.0 * math.sqrt(2.0 / math.pi)
    return x / (1.0 + jnp.exp(x * (k1 + (k1 * 0.044715) * (x * x))))


def _layer_kernel(relb_ref, sink_ref,
                  x_ref, xnext_ref, preg_ref, win_ref, lng_ref, lnb_ref, wsp_ref, bsx_ref, bucket_ref,
                  wout_ref, postg_ref,
                  o_ref,
                  kz0_ref, kz1_ref, va0_ref, va1_ref, wcat_ref, bias_ref,
                  pend_u_ref, pend_vn_ref, pend_ga_ref, pend_gb_ref, pend_q_ref, pend_kv_ref,
                  h_ref):
    f32, bf16 = jnp.float32, jnp.bfloat16
    t = pl.program_id(1)
    first_step = (pl.program_id(0) == 0) & (t == 0)

    row = lax.broadcasted_iota(jnp.int32, (BLOCK, BLOCK), 0)
    col = lax.broadcasted_iota(jnp.int32, (BLOCK, BLOCK), 1)
    tril = col <= row
    low_half = col < HEAD_DIM

    @pl.when(first_step)
    def _init():
        wcol = lax.broadcasted_iota(jnp.int32, (BLOCK, 2 * BLOCK), 1)
        wrow = lax.broadcasted_iota(jnp.int32, (BLOCK, 2 * BLOCK), 0)
        wmask = (wcol % BLOCK) <= wrow
        for p in range(N_PAIRS):
            wcat_ref[p] = jnp.where(wmask, wsp_ref[p], 0.0).astype(bf16)
        bucket = bucket_ref[...]
        for h in range(B_HEADS):
            bias_ref[0, h] = jnp.zeros((BLOCK, BLOCK), f32)

        def fill(b, carry):
            hit = bucket == b
            for h in range(B_HEADS):
                bias_ref[0, h] = jnp.where(hit, relb_ref[b, h], bias_ref[0, h])
            return carry
        lax.fori_loop(0, REL_BUCKETS, fill, 0)
        for h in range(B_HEADS):
            bias_ref[1, h] = jnp.where(tril, bias_ref[0, h], NEG_BIG)

    carry_refs = (kz0_ref, kz1_ref, va0_ref, va1_ref)

    @pl.when(t == 0)
    def _reset_carry():
        for ref in carry_refs:
            ref[...] = jnp.zeros((BLOCK, BLOCK), bf16)

    first_flag = (t == 0).astype(jnp.int32)
    prev = tuple(ref[...] for ref in carry_refs)
    ones_cols = jnp.ones((2 * BLOCK, BLOCK), bf16)

    def in_norm_and_uv(xb):
        ms = jnp.mean(xb * xb, axis=-1, keepdims=True)
        h_ref[...] = (xb * lax.rsqrt(ms + NORM_EPS) * preg_ref[...]).astype(bf16)
        return {"uv": proj(UV_LO, 2 * A_WIDTH)}

    def proj(lo, width):
        tiles = [jnp.dot(h_ref[...], win_ref[:, c:c + V7X_MXU_COLS], preferred_element_type=f32)
                 for c in range(lo, lo + width, V7X_MXU_COLS)]
        return tiles[0] if len(tiles) == 1 else jnp.concatenate(tiles, axis=1)

    def in_gates_and_qkv(st):
        st["q"] = (proj(Q_LO, B_WIDTH) * (HEAD_DIM ** -0.5)).astype(bf16)
        k_v = proj(K_LO, 2 * KV_WIDTH)
        k, vv = k_v[:, :KV_WIDTH], k_v[:, KV_WIDTH:]
        k, vv = k.astype(bf16), vv.astype(bf16)
        zero, one = jnp.zeros((), bf16), jnp.ones((), bf16)
        st["kv"] = (jnp.where(low_half, k, zero), jnp.where(low_half, zero, k),
                    jnp.where(low_half, vv, one), jnp.where(low_half, one, vv))
        st["gate_a"] = jax.nn.silu(proj(AZ_LO, A_WIDTH))
        st["gate_b"] = jax.nn.silu(proj(BZ_LO, B_WIDTH))

    def gelu_and_layer_norm(st):
        uv = _gelu_tanh(st.pop("uv"))
        st["u"], v = uv[:, :A_WIDTH], uv[:, A_WIDTH:]
        mu = jnp.mean(v, axis=-1, keepdims=True)
        vc = v - mu
        var = jnp.mean(vc * vc, axis=-1, keepdims=True)
        vn = vc * lax.rsqrt(var + NORM_EPS) * lng_ref[...] + lnb_ref[...]
        vn = vn.astype(bf16)
        zero = jnp.zeros((), bf16)
        st["vn_rhs"] = [
            jnp.concatenate([jnp.where(low_half, vn[:, p * BLOCK:(p + 1) * BLOCK], zero),
                             jnp.where(low_half, zero, vn[:, p * BLOCK:(p + 1) * BLOCK])],
                            axis=0) for p in range(N_PAIRS)]

    def scores_and_softmax(st, prev, flag):
        q = st.pop("q")
        q_stack = jnp.concatenate([q[:, g * BLOCK:(g + 1) * BLOCK] for g in range(B_GROUP)], axis=0)
        st["probs"], st["row_max"] = [], []
        for kv in range(B_KV_HEADS):
            keys = jnp.concatenate([prev[kv], st["kv"][kv]], axis=0)
            s = lax.dot_general(q_stack, keys, (((1,), (1,)), ((), ())),
                                preferred_element_type=f32)
            probs, row_max = [], []
            for g in range(B_GROUP):
                head = kv * B_GROUP + g
                sg = s[g * BLOCK:(g + 1) * BLOCK, :]
                logits = jnp.where(tril, sg[:, BLOCK:], sg[:, :BLOCK]) + bias_ref[flag, head]
                sink = sink_ref[head]
                m = jnp.maximum(jnp.max(logits, axis=-1, keepdims=True), sink)
                e = jnp.exp(logits - m)
                e = e.astype(bf16)
                zero = jnp.zeros((), bf16)
                probs.append(jnp.concatenate([jnp.where(tril, zero, e), jnp.where(tril, e, zero)], axis=1))
                row_max.append(m)
            st["probs"].append(jnp.concatenate(probs, axis=0))
            st["row_max"].append(row_max)

    def mix_and_attend(st, prev):
        mixed = [jnp.dot(wcat_ref[p], rhs, preferred_element_type=f32)
                 for p, rhs in enumerate(st.pop("vn_rhs"))]
        mixed = jnp.concatenate(mixed, axis=1) + bsx_ref[...]
        y_a = (st.pop("u") * mixed) * st.pop("gate_a")
        pv = []
        for kv in range(B_KV_HEADS):
            vals = jnp.concatenate([jnp.concatenate([prev[2 + kv], st["kv"][2 + kv]], axis=0), ones_cols],
                                   axis=1)
            pv.append(jnp.dot(st["probs"][kv], vals, preferred_element_type=f32))
        attn = []
        for g in range(B_GROUP):
            rows = slice(g * BLOCK, (g + 1) * BLOCK)
            num = jnp.where(low_half, pv[0][rows, :BLOCK], pv[1][rows, :BLOCK])
            row_sum = jnp.where(low_half, pv[0][rows, BLOCK:], pv[1][rows, BLOCK:])
            sink = jnp.where(low_half, sink_ref[g], sink_ref[B_GROUP + g])
            m = jnp.where(low_half, st["row_max"][0][g], st["row_max"][1][g])
            attn.append(num / (row_sum + jnp.exp(sink - m)))
        attn = jnp.concatenate(attn, axis=1)
        y_b = attn * st.pop("gate_b")
        st["y"] = jnp.concatenate([y_a, y_b], axis=1).astype(bf16)

    def out_norm_residual(st, blk):
        r0 = blk * BLOCK
        o = jnp.dot(st.pop("y"), wout_ref[:, :D_MODEL], preferred_element_type=f32)
        os_ = jnp.mean(o * o, axis=-1, keepdims=True)
        o_ref[0, r0:r0 + BLOCK, :] = (x_ref[0, r0:r0 + BLOCK, :]
                                      + o * lax.rsqrt(os_ + NORM_EPS) * postg_ref[...])

    def save_pending(st):
        pend_u_ref[...] = st["u"]
        for p in range(N_PAIRS):
            pend_vn_ref[p] = st["vn_rhs"][p]
        pend_ga_ref[...] = st["gate_a"]
        pend_gb_ref[...] = st["gate_b"]
        pend_q_ref[...] = st["q"]
        for i in range(2 * B_KV_HEADS):
            pend_kv_ref[i] = st["kv"][i]

    def load_pending():
        return {"u": pend_u_ref[...], "vn_rhs": [pend_vn_ref[p] for p in range(N_PAIRS)],
                "gate_a": pend_ga_ref[...], "gate_b": pend_gb_ref[...], "q": pend_q_ref[...],
                "kv": tuple(pend_kv_ref[i] for i in range(2 * B_KV_HEADS))}

    def input_stage(xb):
        st = in_norm_and_uv(xb)
        in_gates_and_qkv(st)
        gelu_and_layer_norm(st)
        return st

    @pl.when(first_step)
    def _prologue():
        save_pending(input_stage(x_ref[0, 0:BLOCK, :]))

    pending = load_pending()
    for blk in range(BLOCKS_PER_STEP):
        nxt = (blk + 1) * BLOCK
        x_next = x_ref[0, nxt:nxt + BLOCK, :] if blk + 1 < BLOCKS_PER_STEP else xnext_ref[0]
        st = in_norm_and_uv(x_next)
        scores_and_softmax(pending, prev, first_flag if blk == 0 else 0)
        in_gates_and_qkv(st)
        gelu_and_layer_norm(st)
        mix_and_attend(pending, prev)
        out_norm_residual(pending, blk)
        prev = pending["kv"]
        pending = st
    save_pending(pending)

    for ref, val in zip(carry_refs, prev):
        ref[...] = val


def _layer(x, rel_bias, sinks, pre_g, w_in, ln_g, ln_b, w_sp, bs_x, bucket, w_out, post_g):
    bsz, seq, _ = x.shape
    steps = seq // TOKENS_PER_STEP
    const2 = lambda b, t: (0, 0)
    const3 = lambda b, t: (0, 0, 0)
    smem = pl.BlockSpec(memory_space=pltpu.SMEM)
    blocks_per_seq = seq // BLOCK
    last_block = bsz * blocks_per_seq - 1

    def next_block_map(b, t):
        n = jnp.minimum(b * blocks_per_seq + (t + 1) * BLOCKS_PER_STEP, last_block)
        return (n // blocks_per_seq, n % blocks_per_seq, 0)

    return pl.pallas_call(
        _layer_kernel,
        grid=(bsz, steps),
        in_specs=[
            smem, smem,
            pl.BlockSpec((1, TOKENS_PER_STEP, D_MODEL), lambda b, t: (b, t, 0)),
            pl.BlockSpec((1, BLOCK, D_MODEL), next_block_map),
            pl.BlockSpec((1, D_MODEL), const2),
            pl.BlockSpec((D_MODEL, IN_WIDTH), const2),
            pl.BlockSpec((1, A_WIDTH), const2),
            pl.BlockSpec((1, A_WIDTH), const2),
            pl.BlockSpec((N_PAIRS, BLOCK, 2 * BLOCK), const3),
            pl.BlockSpec((BLOCK, A_WIDTH), const2),
            pl.BlockSpec((BLOCK, BLOCK), const2),
            pl.BlockSpec((D_MODEL, W_OUT_PITCH), const2),
            pl.BlockSpec((1, D_MODEL), const2),
        ],
        out_specs=pl.BlockSpec((1, TOKENS_PER_STEP, D_MODEL), lambda b, t: (b, t, 0)),
        out_shape=jax.ShapeDtypeStruct(x.shape, x.dtype),
        scratch_shapes=[
            pltpu.VMEM((BLOCK, BLOCK), jnp.bfloat16),
            pltpu.VMEM((BLOCK, BLOCK), jnp.bfloat16),
            pltpu.VMEM((BLOCK, BLOCK), jnp.bfloat16),
            pltpu.VMEM((BLOCK, BLOCK), jnp.bfloat16),
            pltpu.VMEM((N_PAIRS, BLOCK, 2 * BLOCK), jnp.bfloat16),
            pltpu.VMEM((2, B_HEADS, BLOCK, BLOCK), jnp.float32),
            pltpu.VMEM((BLOCK, A_WIDTH), jnp.float32),
            pltpu.VMEM((N_PAIRS, 2 * BLOCK, BLOCK), jnp.bfloat16),
            pltpu.VMEM((BLOCK, A_WIDTH), jnp.float32),
            pltpu.VMEM((BLOCK, B_WIDTH), jnp.float32),
            pltpu.VMEM((BLOCK, B_WIDTH), jnp.bfloat16),
            pltpu.VMEM((2 * B_KV_HEADS, BLOCK, BLOCK), jnp.bfloat16),
            pltpu.VMEM((BLOCK, D_MODEL), jnp.bfloat16),
        ],
        compiler_params=pltpu.CompilerParams(
            dimension_semantics=("arbitrary", "arbitrary"),
            vmem_limit_bytes=_vmem_limit_bytes()),
        name="hybrid_layer",
    )(rel_bias, sinks, x, x, pre_g, w_in, ln_g, ln_b, w_sp, bs_x, bucket, w_out, post_g)


@jax.jit
def kernel(x, pre_norm_g, w_in, ln_v_g, ln_v_b, w_spatial, b_spatial, sinks, rel_bias, w_out, post_norm_g):
    depth = w_in.shape[0]
    bf16 = jnp.bfloat16

    def pair_heads(w, axis):
        shape = w.shape
        w = w.reshape(shape[:axis] + (B_KV_HEADS, B_GROUP, HEAD_DIM) + shape[axis + 1:])
        return jnp.swapaxes(w, axis, axis + 1).reshape(shape)

    w_in_p = jnp.concatenate(
        [w_in[:, :, :Q_LO].astype(bf16), pair_heads(w_in[:, :, Q_LO:K_LO].astype(bf16), 2),
         w_in[:, :, K_LO:BZ_LO].astype(bf16), pair_heads(w_in[:, :, BZ_LO:].astype(bf16), 2)], axis=2)
    w_out_p = jnp.concatenate(
        [w_out[:, :A_WIDTH].astype(bf16), pair_heads(w_out[:, A_WIDTH:].astype(bf16), 1)], axis=1)
    w_out_p = jnp.pad(w_out_p, ((0, 0), (0, 0), (0, W_OUT_PITCH - D_MODEL)))
    w_sp = w_spatial.reshape(depth, N_PAIRS, 2, BLOCK, BLOCK).transpose(0, 1, 3, 2, 4)
    w_sp = w_sp.reshape(depth, N_PAIRS, BLOCK, 2 * BLOCK)
    bs_x = jnp.repeat(jnp.swapaxes(b_spatial, 1, 2), A_GROUP_DIM, axis=2)
    bucket = _combined_bucket_map()
    for layer in range(depth):
        x = _layer(x, rel_bias, sinks[layer], pre_norm_g[layer][None], w_in_p[layer],
                   ln_v_g[layer][None], ln_v_b[layer][None], w_sp[layer], bs_x[layer], bucket,
                   w_out_p[layer], post_norm_g[layer][None])
    return x
```

```python
import math

import jax
import jax.numpy as jnp
from jax import lax
from jax.experimental import pallas as pl
from jax.experimental.pallas import tpu as pltpu

D_MODEL = 1024
A_WIDTH = 512
A_GROUPS = 8
A_GROUP_DIM = A_WIDTH // A_GROUPS
B_HEADS = 8
HEAD_DIM = 64
B_KV_HEADS = 2
B_GROUP = B_HEADS // B_KV_HEADS
B_WIDTH = B_HEADS * HEAD_DIM
KV_WIDTH = B_KV_HEADS * HEAD_DIM
BLOCK = 128
REL_BUCKETS = 32
REL_MAX_DIST = 128
NORM_EPS = 1e-6
IN_WIDTH = 3 * A_WIDTH + 2 * B_WIDTH + 2 * KV_WIDTH
NEG_BIG = -1e30

UV_LO, AZ_LO, Q_LO, K_LO, V_LO, BZ_LO = 0, 2 * A_WIDTH, 3 * A_WIDTH, 3 * A_WIDTH + B_WIDTH, \
    3 * A_WIDTH + B_WIDTH + KV_WIDTH, 3 * A_WIDTH + B_WIDTH + 2 * KV_WIDTH

V7X_LANES = 128
V7X_MXU_COLS = 256
N_PAIRS = A_WIDTH // V7X_LANES
W_OUT_PITCH = D_MODEL + V7X_LANES
BLOCKS_PER_STEP = 8
TOKENS_PER_STEP = BLOCKS_PER_STEP * BLOCK


def _vmem_limit_bytes():
    f32_bytes, bf16_bytes, buffers = 4, 2, 2
    windows = buffers * (
        2 * TOKENS_PER_STEP * D_MODEL * f32_bytes
        + BLOCK * D_MODEL * f32_bytes
        + D_MODEL * IN_WIDTH * bf16_bytes + D_MODEL * W_OUT_PITCH * bf16_bytes
        + (BLOCK * A_WIDTH + BLOCK * BLOCK) * f32_bytes + N_PAIRS * BLOCK * 2 * BLOCK * f32_bytes)
    scratch = ((D_MODEL * IN_WIDTH + D_MODEL * D_MODEL) * bf16_bytes
               + 2 * B_HEADS * BLOCK * BLOCK * f32_bytes
               + 3 * BLOCK * A_WIDTH * f32_bytes
               + (N_PAIRS * 2 + B_WIDTH // BLOCK + 4 + 4 + D_MODEL // BLOCK + N_PAIRS * 2)
               * BLOCK * BLOCK * bf16_bytes)
    temporaries = 2 * BLOCK * IN_WIDTH * f32_bytes * BLOCKS_PER_STEP // 2
    return windows + scratch + temporaries


def _combined_bucket_map():
    i = jnp.arange(BLOCK)[:, None]
    j = jnp.arange(BLOCK)[None, :]
    dist = jnp.where(j <= i, i - j, i + BLOCK - j)
    max_exact = REL_BUCKETS // 2
    safe = jnp.maximum(dist, 1).astype(jnp.float32)
    large = max_exact + (jnp.log(safe / max_exact) / math.log(REL_MAX_DIST / max_exact)
                         * (REL_BUCKETS - max_exact)).astype(jnp.int32)
    large = jnp.minimum(large, REL_BUCKETS - 1)
    return jnp.where(dist < max_exact, dist, large).astype(jnp.int32)


def _gelu_tanh(x):
    k1 = -2.0 * math.sqrt(2.0 / math.pi)
    return x / (1.0 + jnp.exp(x * (k1 + (k1 * 0.044715) * (x * x))))


def _layer_kernel(relb_ref, sink_ref,
                  x_ref, xnext_ref, preg_ref, win_ref, lng_ref, lnb_ref, wsp_ref, bsx_ref, bucket_ref,
                  wout_ref, postg_ref,
                  o_ref,
                  kz0_ref, kz1_ref, va0_ref, va1_ref, wcat_ref, bias_ref,
                  pend_u_ref, pend_vn_ref, pend_ga_ref, pend_gb_ref, pend_q_ref, pend_kv_ref,
                  h_ref, win_s_ref, wout_s_ref):
    f32, bf16 = jnp.float32, jnp.bfloat16
    t = pl.program_id(1)
    first_step = (pl.program_id(0) == 0) & (t == 0)

    row = lax.broadcasted_iota(jnp.int32, (BLOCK, BLOCK), 0)
    col = lax.broadcasted_iota(jnp.int32, (BLOCK, BLOCK), 1)
    tril = col <= row
    low_half = col < HEAD_DIM

    @pl.when(first_step)
    def _init():
        for r in range(0, D_MODEL, BLOCK):
            win_s_ref[r:r + BLOCK, :] = win_ref[r:r + BLOCK, :]
            wout_s_ref[r:r + BLOCK, :] = wout_ref[r:r + BLOCK, :D_MODEL]
        wcol = lax.broadcasted_iota(jnp.int32, (BLOCK, 2 * BLOCK), 1)
        wrow = lax.broadcasted_iota(jnp.int32, (BLOCK, 2 * BLOCK), 0)
        wmask = (wcol % BLOCK) <= wrow
        for p in range(N_PAIRS):
            wcat_ref[p] = jnp.where(wmask, wsp_ref[p], 0.0).astype(bf16)
        bucket = bucket_ref[...]
        for h in range(B_HEADS):
            bias_ref[0, h] = jnp.zeros((BLOCK, BLOCK), f32)

        def fill(b, carry):
            hit = bucket == b
            for h in range(B_HEADS):
                bias_ref[0, h] = jnp.where(hit, relb_ref[b, h], bias_ref[0, h])
            return carry
        lax.fori_loop(0, REL_BUCKETS, fill, 0)
        for h in range(B_HEADS):
            bias_ref[1, h] = jnp.where(tril, bias_ref[0, h], NEG_BIG)

    carry_refs = (kz0_ref, kz1_ref, va0_ref, va1_ref)

    @pl.when(t == 0)
    def _reset_carry():
        for ref in carry_refs:
            ref[...] = jnp.zeros((BLOCK, BLOCK), bf16)

    first_flag = (t == 0).astype(jnp.int32)
    prev = tuple(ref[...] for ref in carry_refs)
    ones_cols = jnp.ones((2 * BLOCK, BLOCK), bf16)

    def in_norm_and_uv(xb):
        ms = jnp.mean(xb * xb, axis=-1, keepdims=True)
        h_ref[...] = (xb * lax.rsqrt(ms + NORM_EPS) * preg_ref[...]).astype(bf16)
        return {"uv": proj(UV_LO, 2 * A_WIDTH)}

    def proj(lo, width):
        tiles = [jnp.dot(h_ref[...], win_s_ref[:, c:c + V7X_MXU_COLS], preferred_element_type=f32)
                 for c in range(lo, lo + width, V7X_MXU_COLS)]
        return tiles[0] if len(tiles) == 1 else jnp.concatenate(tiles, axis=1)

    def in_gates_and_qkv(st):
        st["q"] = (proj(Q_LO, B_WIDTH) * (HEAD_DIM ** -0.5)).astype(bf16)
        k_v = proj(K_LO, 2 * KV_WIDTH)
        k, vv = k_v[:, :KV_WIDTH], k_v[:, KV_WIDTH:]
        k, vv = k.astype(bf16), vv.astype(bf16)
        zero, one = jnp.zeros((), bf16), jnp.ones((), bf16)
        st["kv"] = (jnp.where(low_half, k, zero), jnp.where(low_half, zero, k),
                    jnp.where(low_half, vv, one), jnp.where(low_half, one, vv))
        st["gate_a"] = jax.nn.silu(proj(AZ_LO, A_WIDTH))
        st["gate_b"] = jax.nn.silu(proj(BZ_LO, B_WIDTH))

    def gelu_and_layer_norm(st):
        uv = _gelu_tanh(st.pop("uv"))
        st["u"], v = uv[:, :A_WIDTH], uv[:, A_WIDTH:]
        mu = jnp.mean(v, axis=-1, keepdims=True)
        vc = v - mu
        var = jnp.mean(vc * vc, axis=-1, keepdims=True)
        vn = vc * lax.rsqrt(var + NORM_EPS) * lng_ref[...] + lnb_ref[...]
        vn = vn.astype(bf16)
        zero = jnp.zeros((), bf16)
        st["vn_rhs"] = [
            jnp.concatenate([jnp.where(low_half, vn[:, p * BLOCK:(p + 1) * BLOCK], zero),
                             jnp.where(low_half, zero, vn[:, p * BLOCK:(p + 1) * BLOCK])],
                            axis=0) for p in range(N_PAIRS)]

    def scores_and_softmax(st, prev, flag):
        q = st.pop("q")
        q_stack = jnp.concatenate([q[:, g * BLOCK:(g + 1) * BLOCK] for g in range(B_GROUP)], axis=0)
        st["probs"], st["row_max"] = [], []
        for kv in range(B_KV_HEADS):
            keys = jnp.concatenate([prev[kv], st["kv"][kv]], axis=0)
            s = lax.dot_general(q_stack, keys, (((1,), (1,)), ((), ())),
                                preferred_element_type=f32)
            probs, row_max = [], []
            for g in range(B_GROUP):
                head = kv * B_GROUP + g
                sg = s[g * BLOCK:(g + 1) * BLOCK, :]
                logits = jnp.where(tril, sg[:, BLOCK:], sg[:, :BLOCK]) + bias_ref[flag, head]
                sink = sink_ref[head]
                m = jnp.maximum(jnp.max(logits, axis=-1, keepdims=True), sink)
                e = jnp.exp(logits - m)
                e = e.astype(bf16)
                zero = jnp.zeros((), bf16)
                probs.append(jnp.concatenate([jnp.where(tril, zero, e), jnp.where(tril, e, zero)], axis=1))
                row_max.append(m)
            st["probs"].append(jnp.concatenate(probs, axis=0))
            st["row_max"].append(row_max)

    def mix_and_attend(st, prev):
        mixed = [jnp.dot(wcat_ref[p], rhs, preferred_element_type=f32)
                 for p, rhs in enumerate(st.pop("vn_rhs"))]
        mixed = jnp.concatenate(mixed, axis=1) + bsx_ref[...]
        y_a = (st.pop("u") * mixed) * st.pop("gate_a")
        pv = []
        for kv in range(B_KV_HEADS):
            vals = jnp.concatenate([jnp.concatenate([prev[2 + kv], st["kv"][2 + kv]], axis=0), ones_cols],
                                   axis=1)
            pv.append(jnp.dot(st["probs"][kv], vals, preferred_element_type=f32))
        attn = []
        for g in range(B_GROUP):
            rows = slice(g * BLOCK, (g + 1) * BLOCK)
            num = jnp.where(low_half, pv[0][rows, :BLOCK], pv[1][rows, :BLOCK])
            row_sum = jnp.where(low_half, pv[0][rows, BLOCK:], pv[1][rows, BLOCK:])
            sink = jnp.where(low_half, sink_ref[g], sink_ref[B_GROUP + g])
            m = jnp.where(low_half, st["row_max"][0][g], st["row_max"][1][g])
            attn.append(num / (row_sum + jnp.exp(sink - m)))
        attn = jnp.concatenate(attn, axis=1)
        y_b = attn * st.pop("gate_b")
        st["y"] = jnp.concatenate([y_a, y_b], axis=1).astype(bf16)

    def out_norm_residual(st, blk):
        r0 = blk * BLOCK
        o = jnp.dot(st.pop("y"), wout_s_ref[...], preferred_element_type=f32)
        os_ = jnp.mean(o * o, axis=-1, keepdims=True)
        o_ref[0, r0:r0 + BLOCK, :] = (x_ref[0, r0:r0 + BLOCK, :]
                                      + o * lax.rsqrt(os_ + NORM_EPS) * postg_ref[...])

    def save_pending(st):
        pend_u_ref[...] = st["u"]
        for p in range(N_PAIRS):
            pend_vn_ref[p] = st["vn_rhs"][p]
        pend_ga_ref[...] = st["gate_a"]
        pend_gb_ref[...] = st["gate_b"]
        pend_q_ref[...] = st["q"]
        for i in range(2 * B_KV_HEADS):
            pend_kv_ref[i] = st["kv"][i]

    def load_pending():
        return {"u": pend_u_ref[...], "vn_rhs": [pend_vn_ref[p] for p in range(N_PAIRS)],
                "gate_a": pend_ga_ref[...], "gate_b": pend_gb_ref[...], "q": pend_q_ref[...],
                "kv": tuple(pend_kv_ref[i] for i in range(2 * B_KV_HEADS))}

    def input_stage(xb):
        st = in_norm_and_uv(xb)
        in_gates_and_qkv(st)
        gelu_and_layer_norm(st)
        return st

    @pl.when(first_step)
    def _prologue():
        save_pending(input_stage(x_ref[0, 0:BLOCK, :]))

    pending = load_pending()
    for blk in range(BLOCKS_PER_STEP):
        nxt = (blk + 1) * BLOCK
        x_next = x_ref[0, nxt:nxt + BLOCK, :] if blk + 1 < BLOCKS_PER_STEP else xnext_ref[0]
        st = in_norm_and_uv(x_next)
        scores_and_softmax(pending, prev, first_flag if blk == 0 else 0)
        in_gates_and_qkv(st)
        gelu_and_layer_norm(st)
        mix_and_attend(pending, prev)
        out_norm_residual(pending, blk)
        prev = pending["kv"]
        pending = st
    save_pending(pending)

    for ref, val in zip(carry_refs, prev):
        ref[...] = val


def _layer(x, rel_bias, sinks, pre_g, w_in, ln_g, ln_b, w_sp, bs_x, bucket, w_out, post_g):
    bsz, seq, _ = x.shape
    steps = seq // TOKENS_PER_STEP
    const2 = lambda b, t: (0, 0)
    const3 = lambda b, t: (0, 0, 0)
    smem = pl.BlockSpec(memory_space=pltpu.SMEM)
    blocks_per_seq = seq // BLOCK
    last_block = bsz * blocks_per_seq - 1

    def next_block_map(b, t):
        n = jnp.minimum(b * blocks_per_seq + (t + 1) * BLOCKS_PER_STEP, last_block)
        return (n // blocks_per_seq, n % blocks_per_seq, 0)

    return pl.pallas_call(
        _layer_kernel,
        grid=(bsz, steps),
        in_specs=[
            smem, smem,
            pl.BlockSpec((1, TOKENS_PER_STEP, D_MODEL), lambda b, t: (b, t, 0)),
            pl.BlockSpec((1, BLOCK, D_MODEL), next_block_map),
            pl.BlockSpec((1, D_MODEL), const2),
            pl.BlockSpec((D_MODEL, IN_WIDTH), const2),
            pl.BlockSpec((1, A_WIDTH), const2),
            pl.BlockSpec((1, A_WIDTH), const2),
            pl.BlockSpec((N_PAIRS, BLOCK, 2 * BLOCK), const3),
            pl.BlockSpec((BLOCK, A_WIDTH), const2),
            pl.BlockSpec((BLOCK, BLOCK), const2),
            pl.BlockSpec((D_MODEL, W_OUT_PITCH), const2),
            pl.BlockSpec((1, D_MODEL), const2),
        ],
        out_specs=pl.BlockSpec((1, TOKENS_PER_STEP, D_MODEL), lambda b, t: (b, t, 0)),
        out_shape=jax.ShapeDtypeStruct(x.shape, x.dtype),
        scratch_shapes=[
            pltpu.VMEM((BLOCK, BLOCK), jnp.bfloat16),
            pltpu.VMEM((BLOCK, BLOCK), jnp.bfloat16),
            pltpu.VMEM((BLOCK, BLOCK), jnp.bfloat16),
            pltpu.VMEM((BLOCK, BLOCK), jnp.bfloat16),
            pltpu.VMEM((N_PAIRS, BLOCK, 2 * BLOCK), jnp.bfloat16),
            pltpu.VMEM((2, B_HEADS, BLOCK, BLOCK), jnp.float32),
            pltpu.VMEM((BLOCK, A_WIDTH), jnp.float32),
            pltpu.VMEM((N_PAIRS, 2 * BLOCK, BLOCK), jnp.bfloat16),
            pltpu.VMEM((BLOCK, A_WIDTH), jnp.float32),
            pltpu.VMEM((BLOCK, B_WIDTH), jnp.float32),
            pltpu.VMEM((BLOCK, B_WIDTH), jnp.bfloat16),
            pltpu.VMEM((2 * B_KV_HEADS, BLOCK, BLOCK), jnp.bfloat16),
            pltpu.VMEM((BLOCK, D_MODEL), jnp.bfloat16),
            pltpu.VMEM((D_MODEL, IN_WIDTH), jnp.bfloat16),
            pltpu.VMEM((D_MODEL, D_MODEL), jnp.bfloat16),
        ],
        compiler_params=pltpu.CompilerParams(
            dimension_semantics=("arbitrary", "arbitrary"),
            vmem_limit_bytes=_vmem_limit_bytes()),
        name="hybrid_layer",
    )(rel_bias, sinks, x, x, pre_g, w_in, ln_g, ln_b, w_sp, bs_x, bucket, w_out, post_g)


@jax.jit
def kernel(x, pre_norm_g, w_in, ln_v_g, ln_v_b, w_spatial, b_spatial, sinks, rel_bias, w_out, post_norm_g):
    depth = w_in.shape[0]
    bf16 = jnp.bfloat16

    def pair_heads(w, axis):
        shape = w.shape
        w = w.reshape(shape[:axis] + (B_KV_HEADS, B_GROUP, HEAD_DIM) + shape[axis + 1:])
        return jnp.swapaxes(w, axis, axis + 1).reshape(shape)

    w_in_p = jnp.concatenate(
        [w_in[:, :, :Q_LO].astype(bf16), pair_heads(w_in[:, :, Q_LO:K_LO].astype(bf16), 2),
         w_in[:, :, K_LO:BZ_LO].astype(bf16), pair_heads(w_in[:, :, BZ_LO:].astype(bf16), 2)], axis=2)
    w_out_p = jnp.concatenate(
        [w_out[:, :A_WIDTH].astype(bf16), pair_heads(w_out[:, A_WIDTH:].astype(bf16), 1)], axis=1)
    w_out_p = jnp.pad(w_out_p, ((0, 0), (0, 0), (0, W_OUT_PITCH - D_MODEL)))
    w_sp = w_spatial.reshape(depth, N_PAIRS, 2, BLOCK, BLOCK).transpose(0, 1, 3, 2, 4)
    w_sp = w_sp.reshape(depth, N_PAIRS, BLOCK, 2 * BLOCK)
    bs_x = jnp.repeat(jnp.swapaxes(b_spatial, 1, 2), A_GROUP_DIM, axis=2)
    bucket = _combined_bucket_map()
    for layer in range(depth):
        x = _layer(x, rel_bias, sinks[layer], pre_norm_g[layer][None], w_in_p[layer],
                   ln_v_g[layer][None], ln_v_b[layer][None], w_sp[layer], bs_x[layer], bucket,
                   w_out_p[layer], post_norm_g[layer][None])
    return x
```

```python
import math

import jax
import jax.numpy as jnp
from jax import lax
from jax.experimental import pallas as pl
from jax.experimental.pallas import tpu as pltpu

D_MODEL = 1024
A_WIDTH = 512
A_GROUPS = 8
A_GROUP_DIM = A_WIDTH // A_GROUPS
B_HEADS = 8
HEAD_DIM = 64
B_KV_HEADS = 2
B_GROUP = B_HEADS // B_KV_HEADS
B_WIDTH = B_HEADS * HEAD_DIM
KV_WIDTH = B_KV_HEADS * HEAD_DIM
BLOCK = 128
REL_BUCKETS = 32
REL_MAX_DIST = 128
NORM_EPS = 1e-6
IN_WIDTH = 3 * A_WIDTH + 2 * B_WIDTH + 2 * KV_WIDTH
NEG_BIG = -1e30

UV_LO, AZ_LO, Q_LO, K_LO, V_LO, BZ_LO = 0, 2 * A_WIDTH, 3 * A_WIDTH, 3 * A_WIDTH + B_WIDTH, \
    3 * A_WIDTH + B_WIDTH + KV_WIDTH, 3 * A_WIDTH + B_WIDTH + 2 * KV_WIDTH

V7X_LANES = 128
V7X_MXU_COLS = 256
N_PAIRS = A_WIDTH // V7X_LANES
BLOCKS_PER_STEP = 8
TOKENS_PER_STEP = BLOCKS_PER_STEP * BLOCK


def _vmem_limit_bytes():
    f32_bytes, bf16_bytes, buffers = 4, 2, 2
    windows = buffers * (
        2 * TOKENS_PER_STEP * D_MODEL * f32_bytes
        + BLOCK * D_MODEL * f32_bytes
        + D_MODEL * IN_WIDTH * bf16_bytes + D_MODEL * D_MODEL * bf16_bytes
        + (BLOCK * A_WIDTH + BLOCK * BLOCK) * f32_bytes + N_PAIRS * BLOCK * 2 * BLOCK * f32_bytes)
    scratch = ((D_MODEL * IN_WIDTH + D_MODEL * D_MODEL) * bf16_bytes
               + 2 * B_HEADS * BLOCK * BLOCK * f32_bytes
               + 3 * BLOCK * A_WIDTH * f32_bytes
               + (N_PAIRS * 2 + B_WIDTH // BLOCK + 4 + 4 + D_MODEL // BLOCK + N_PAIRS * 2)
               * BLOCK * BLOCK * bf16_bytes)
    temporaries = 2 * BLOCK * IN_WIDTH * f32_bytes * BLOCKS_PER_STEP // 2
    return windows + scratch + temporaries


def _combined_bucket_map():
    i = jnp.arange(BLOCK)[:, None]
    j = jnp.arange(BLOCK)[None, :]
    dist = jnp.where(j <= i, i - j, i + BLOCK - j)
    max_exact = REL_BUCKETS // 2
    safe = jnp.maximum(dist, 1).astype(jnp.float32)
    large = max_exact + (jnp.log(safe / max_exact) / math.log(REL_MAX_DIST / max_exact)
                         * (REL_BUCKETS - max_exact)).astype(jnp.int32)
    large = jnp.minimum(large, REL_BUCKETS - 1)
    return jnp.where(dist < max_exact, dist, large).astype(jnp.int32)


def _gelu_tanh(x):
    k1 = -2.0 * math.sqrt(2.0 / math.pi)
    return x / (1.0 + jnp.exp(x * (k1 + (k1 * 0.044715) * (x * x))))


def _layer_kernel(relb_ref, sink_ref,
                  x_ref, xnext_ref, preg_ref, win_ref, lng_ref, lnb_ref, wsp_ref, bsx_ref, bucket_ref,
                  wout_ref, postg_ref,
                  o_ref,
                  kz0_ref, kz1_ref, va0_ref, va1_ref, wcat_ref, bias_ref,
                  pend_u_ref, pend_vn_ref, pend_ga_ref, pend_gb_ref, pend_q_ref, pend_kv_ref,
                  h_ref, win_s_ref, wout_s_ref):
    f32, bf16 = jnp.float32, jnp.bfloat16
    t = pl.program_id(1)
    first_step = (pl.program_id(0) == 0) & (t == 0)

    row = lax.broadcasted_iota(jnp.int32, (BLOCK, BLOCK), 0)
    col = lax.broadcasted_iota(jnp.int32, (BLOCK, BLOCK), 1)
    tril = col <= row
    low_half = col < HEAD_DIM

    @pl.when(first_step)
    def _init():
        half_packed = BLOCK // 2
        low_lanes = lax.broadcasted_iota(jnp.int32, (half_packed, BLOCK), 1) < HEAD_DIM

        def pair_head_columns(chunk):
            words = pltpu.bitcast(chunk, jnp.uint32)
            old = [words[:, s * BLOCK:(s + 1) * BLOCK] for s in range(N_PAIRS)]
            swapped = [pltpu.roll(slab, HEAD_DIM, axis=1) for slab in old]
            new = []
            for g in range(B_GROUP):
                src0, src1 = g // 2, B_GROUP // 2 + g // 2
                if g % 2 == 0:
                    new.append(jnp.where(low_lanes, old[src0], swapped[src1]))
                else:
                    new.append(jnp.where(low_lanes, swapped[src0], old[src1]))
            return pltpu.bitcast(jnp.concatenate(new, axis=1), bf16)

        for r in range(0, D_MODEL, BLOCK):
            rows = slice(r, r + BLOCK)
            win_s_ref[rows, :Q_LO] = win_ref[rows, :Q_LO]
            win_s_ref[rows, Q_LO:K_LO] = pair_head_columns(win_ref[rows, Q_LO:K_LO])
            win_s_ref[rows, K_LO:BZ_LO] = win_ref[rows, K_LO:BZ_LO]
            win_s_ref[rows, BZ_LO:] = pair_head_columns(win_ref[rows, BZ_LO:])
        wout_s_ref[:A_WIDTH, :] = wout_ref[:A_WIDTH, :]
        for g in range(B_GROUP):
            for kv in range(B_KV_HEADS):
                dst = A_WIDTH + (g * B_KV_HEADS + kv) * HEAD_DIM
                src = A_WIDTH + (kv * B_GROUP + g) * HEAD_DIM
                wout_s_ref[dst:dst + HEAD_DIM, :] = wout_ref[src:src + HEAD_DIM, :]
        wcol = lax.broadcasted_iota(jnp.int32, (BLOCK, 2 * BLOCK), 1)
        wrow = lax.broadcasted_iota(jnp.int32, (BLOCK, 2 * BLOCK), 0)
        wmask = (wcol % BLOCK) <= wrow
        for p in range(N_PAIRS):
            wcat_ref[p] = jnp.where(wmask, wsp_ref[p], 0.0).astype(bf16)
        bucket = bucket_ref[...]
        for h in range(B_HEADS):
            bias_ref[0, h] = jnp.zeros((BLOCK, BLOCK), f32)

        def fill(b, carry):
            hit = bucket == b
            for h in range(B_HEADS):
                bias_ref[0, h] = jnp.where(hit, relb_ref[b, h], bias_ref[0, h])
            return carry
        lax.fori_loop(0, REL_BUCKETS, fill, 0)
        for h in range(B_HEADS):
            bias_ref[1, h] = jnp.where(tril, bias_ref[0, h], NEG_BIG)

    carry_refs = (kz0_ref, kz1_ref, va0_ref, va1_ref)

    @pl.when(t == 0)
    def _reset_carry():
        for ref in carry_refs:
            ref[...] = jnp.zeros((BLOCK, BLOCK), bf16)

    first_flag = (t == 0).astype(jnp.int32)
    prev = tuple(ref[...] for ref in carry_refs)
    ones_cols = jnp.ones((2 * BLOCK, BLOCK), bf16)

    def in_norm_and_uv(xb):
        ms = jnp.mean(xb * xb, axis=-1, keepdims=True)
        h_ref[...] = (xb * lax.rsqrt(ms + NORM_EPS) * preg_ref[...]).astype(bf16)
        return {"uv": proj(UV_LO, 2 * A_WIDTH)}

    def proj(lo, width):
        tiles = [jnp.dot(h_ref[...], win_s_ref[:, c:c + V7X_MXU_COLS], preferred_element_type=f32)
                 for c in range(lo, lo + width, V7X_MXU_COLS)]
        return tiles[0] if len(tiles) == 1 else jnp.concatenate(tiles, axis=1)

    def in_gates_and_qkv(st):
        st["q"] = (proj(Q_LO, B_WIDTH) * (HEAD_DIM ** -0.5)).astype(bf16)
        k_v = proj(K_LO, 2 * KV_WIDTH)
        k, vv = k_v[:, :KV_WIDTH], k_v[:, KV_WIDTH:]
        k, vv = k.astype(bf16), vv.astype(bf16)
        zero, one = jnp.zeros((), bf16), jnp.ones((), bf16)
        st["kv"] = (jnp.where(low_half, k, zero), jnp.where(low_half, zero, k),
                    jnp.where(low_half, vv, one), jnp.where(low_half, one, vv))
        st["gate_a"] = jax.nn.silu(proj(AZ_LO, A_WIDTH))
        st["gate_b"] = jax.nn.silu(proj(BZ_LO, B_WIDTH))

    def gelu_and_layer_norm(st):
        uv = _gelu_tanh(st.pop("uv"))
        st["u"], v = uv[:, :A_WIDTH], uv[:, A_WIDTH:]
        mu = jnp.mean(v, axis=-1, keepdims=True)
        vc = v - mu
        var = jnp.mean(vc * vc, axis=-1, keepdims=True)
        vn = vc * lax.rsqrt(var + NORM_EPS) * lng_ref[...] + lnb_ref[...]
        vn = vn.astype(bf16)
        zero = jnp.zeros((), bf16)
        st["vn_rhs"] = [
            jnp.concatenate([jnp.where(low_half, vn[:, p * BLOCK:(p + 1) * BLOCK], zero),
                             jnp.where(low_half, zero, vn[:, p * BLOCK:(p + 1) * BLOCK])],
                            axis=0) for p in range(N_PAIRS)]

    def scores_and_softmax(st, prev, flag):
        q = st.pop("q")
        q_stack = jnp.concatenate([q[:, g * BLOCK:(g + 1) * BLOCK] for g in range(B_GROUP)], axis=0)
        st["probs"], st["row_max"] = [], []
        for kv in range(B_KV_HEADS):
            keys = jnp.concatenate([prev[kv], st["kv"][kv]], axis=0)
            s = lax.dot_general(q_stack, keys, (((1,), (1,)), ((), ())),
                                preferred_element_type=f32)
            probs, row_max = [], []
            for g in range(B_GROUP):
                head = kv * B_GROUP + g
                sg = s[g * BLOCK:(g + 1) * BLOCK, :]
                logits = jnp.where(tril, sg[:, BLOCK:], sg[:, :BLOCK]) + bias_ref[flag, head]
                sink = sink_ref[head]
                m = jnp.maximum(jnp.max(logits, axis=-1, keepdims=True), sink)
                e = jnp.exp(logits - m)
                e = e.astype(bf16)
                zero = jnp.zeros((), bf16)
                probs.append(jnp.concatenate([jnp.where(tril, zero, e), jnp.where(tril, e, zero)], axis=1))
                row_max.append(m)
            st["probs"].append(jnp.concatenate(probs, axis=0))
            st["row_max"].append(row_max)

    def mix_and_attend(st, prev):
        mixed = [jnp.dot(wcat_ref[p], rhs, preferred_element_type=f32)
                 for p, rhs in enumerate(st.pop("vn_rhs"))]
        mixed = jnp.concatenate(mixed, axis=1) + bsx_ref[...]
        y_a = (st.pop("u") * mixed) * st.pop("gate_a")
        pv = []
        for kv in range(B_KV_HEADS):
            vals = jnp.concatenate([jnp.concatenate([prev[2 + kv], st["kv"][2 + kv]], axis=0), ones_cols],
                                   axis=1)
            pv.append(jnp.dot(st["probs"][kv], vals, preferred_element_type=f32))
        attn = []
        for g in range(B_GROUP):
            rows = slice(g * BLOCK, (g + 1) * BLOCK)
            num = jnp.where(low_half, pv[0][rows, :BLOCK], pv[1][rows, :BLOCK])
            row_sum = jnp.where(low_half, pv[0][rows, BLOCK:], pv[1][rows, BLOCK:])
            sink = jnp.where(low_half, sink_ref[g], sink_ref[B_GROUP + g])
            m = jnp.where(low_half, st["row_max"][0][g], st["row_max"][1][g])
            attn.append(num / (row_sum + jnp.exp(sink - m)))
        attn = jnp.concatenate(attn, axis=1)
        y_b = attn * st.pop("gate_b")
        st["y"] = jnp.concatenate([y_a, y_b], axis=1).astype(bf16)

    def out_norm_residual(st, blk):
        r0 = blk * BLOCK
        o = jnp.dot(st.pop("y"), wout_s_ref[...], preferred_element_type=f32)
        os_ = jnp.mean(o * o, axis=-1, keepdims=True)
        o_ref[0, r0:r0 + BLOCK, :] = (x_ref[0, r0:r0 + BLOCK, :]
                                      + o * lax.rsqrt(os_ + NORM_EPS) * postg_ref[...])

    def save_pending(st):
        pend_u_ref[...] = st["u"]
        for p in range(N_PAIRS):
            pend_vn_ref[p] = st["vn_rhs"][p]
        pend_ga_ref[...] = st["gate_a"]
        pend_gb_ref[...] = st["gate_b"]
        pend_q_ref[...] = st["q"]
        for i in range(2 * B_KV_HEADS):
            pend_kv_ref[i] = st["kv"][i]

    def load_pending():
        return {"u": pend_u_ref[...], "vn_rhs": [pend_vn_ref[p] for p in range(N_PAIRS)],
                "gate_a": pend_ga_ref[...], "gate_b": pend_gb_ref[...], "q": pend_q_ref[...],
                "kv": tuple(pend_kv_ref[i] for i in range(2 * B_KV_HEADS))}

    def input_stage(xb):
        st = in_norm_and_uv(xb)
        in_gates_and_qkv(st)
        gelu_and_layer_norm(st)
        return st

    @pl.when(first_step)
    def _prologue():
        save_pending(input_stage(x_ref[0, 0:BLOCK, :]))

    pending = load_pending()
    for blk in range(BLOCKS_PER_STEP):
        nxt = (blk + 1) * BLOCK
        x_next = x_ref[0, nxt:nxt + BLOCK, :] if blk + 1 < BLOCKS_PER_STEP else xnext_ref[0]
        st = in_norm_and_uv(x_next)
        scores_and_softmax(pending, prev, first_flag if blk == 0 else 0)
        in_gates_and_qkv(st)
        gelu_and_layer_norm(st)
        mix_and_attend(pending, prev)
        out_norm_residual(pending, blk)
        prev = pending["kv"]
        pending = st
    save_pending(pending)

    for ref, val in zip(carry_refs, prev):
        ref[...] = val


def _layer(x, rel_bias, sinks, pre_g, w_in, ln_g, ln_b, w_sp, bs_x, bucket, w_out, post_g):
    bsz, seq, _ = x.shape
    steps = seq // TOKENS_PER_STEP
    const2 = lambda b, t: (0, 0)
    const3 = lambda b, t: (0, 0, 0)
    smem = pl.BlockSpec(memory_space=pltpu.SMEM)
    blocks_per_seq = seq // BLOCK
    last_block = bsz * blocks_per_seq - 1

    def next_block_map(b, t):
        n = jnp.minimum(b * blocks_per_seq + (t + 1) * BLOCKS_PER_STEP, last_block)
        return (n // blocks_per_seq, n % blocks_per_seq, 0)

    return pl.pallas_call(
        _layer_kernel,
        grid=(bsz, steps),
        in_specs=[
            smem, smem,
            pl.BlockSpec((1, TOKENS_PER_STEP, D_MODEL), lambda b, t: (b, t, 0)),
            pl.BlockSpec((1, BLOCK, D_MODEL), next_block_map),
            pl.BlockSpec((1, D_MODEL), const2),
            pl.BlockSpec((D_MODEL, IN_WIDTH), const2),
            pl.BlockSpec((1, A_WIDTH), const2),
            pl.BlockSpec((1, A_WIDTH), const2),
            pl.BlockSpec((N_PAIRS, BLOCK, 2 * BLOCK), const3),
            pl.BlockSpec((BLOCK, A_WIDTH), const2),
            pl.BlockSpec((BLOCK, BLOCK), const2),
            pl.BlockSpec((D_MODEL, D_MODEL), const2),
            pl.BlockSpec((1, D_MODEL), const2),
        ],
        out_specs=pl.BlockSpec((1, TOKENS_PER_STEP, D_MODEL), lambda b, t: (b, t, 0)),
        out_shape=jax.ShapeDtypeStruct(x.shape, x.dtype),
        scratch_shapes=[
            pltpu.VMEM((BLOCK, BLOCK), jnp.bfloat16),
            pltpu.VMEM((BLOCK, BLOCK), jnp.bfloat16),
            pltpu.VMEM((BLOCK, BLOCK), jnp.bfloat16),
            pltpu.VMEM((BLOCK, BLOCK), jnp.bfloat16),
            pltpu.VMEM((N_PAIRS, BLOCK, 2 * BLOCK), jnp.bfloat16),
            pltpu.VMEM((2, B_HEADS, BLOCK, BLOCK), jnp.float32),
            pltpu.VMEM((BLOCK, A_WIDTH), jnp.float32),
            pltpu.VMEM((N_PAIRS, 2 * BLOCK, BLOCK), jnp.bfloat16),
            pltpu.VMEM((BLOCK, A_WIDTH), jnp.float32),
            pltpu.VMEM((BLOCK, B_WIDTH), jnp.float32),
            pltpu.VMEM((BLOCK, B_WIDTH), jnp.bfloat16),
            pltpu.VMEM((2 * B_KV_HEADS, BLOCK, BLOCK), jnp.bfloat16),
            pltpu.VMEM((BLOCK, D_MODEL), jnp.bfloat16),
            pltpu.VMEM((D_MODEL, IN_WIDTH), jnp.bfloat16),
            pltpu.VMEM((D_MODEL, D_MODEL), jnp.bfloat16),
        ],
        compiler_params=pltpu.CompilerParams(
            dimension_semantics=("arbitrary", "arbitrary"),
            vmem_limit_bytes=_vmem_limit_bytes()),
        name="hybrid_layer",
    )(rel_bias, sinks, x, x, pre_g, w_in, ln_g, ln_b, w_sp, bs_x, bucket, w_out, post_g)


@jax.jit
def kernel(x, pre_norm_g, w_in, ln_v_g, ln_v_b, w_spatial, b_spatial, sinks, rel_bias, w_out, post_norm_g):
    depth = w_in.shape[0]
    w_in_p = w_in.astype(jnp.bfloat16)
    w_out_p = w_out.astype(jnp.bfloat16)
    w_sp = w_spatial.reshape(depth, N_PAIRS, 2, BLOCK, BLOCK).transpose(0, 1, 3, 2, 4)
    w_sp = w_sp.reshape(depth, N_PAIRS, BLOCK, 2 * BLOCK)
    bs_x = jnp.repeat(jnp.swapaxes(b_spatial, 1, 2), A_GROUP_DIM, axis=2)
    bucket = _combined_bucket_map()
    for layer in range(depth):
        x = _layer(x, rel_bias, sinks[layer], pre_norm_g[layer][None], w_in_p[layer],
                   ln_v_g[layer][None], ln_v_b[layer][None], w_sp[layer], bs_x[layer], bucket,
                   w_out_p[layer], post_norm_g[layer][None])
    return x
```

```python
import math

import jax
import jax.numpy as jnp
from jax import lax
from jax.experimental import pallas as pl
from jax.experimental.pallas import tpu as pltpu

D_MODEL = 1024
A_WIDTH = 512
A_GROUPS = 8
A_GROUP_DIM = A_WIDTH // A_GROUPS
B_HEADS = 8
HEAD_DIM = 64
B_KV_HEADS = 2
B_GROUP = B_HEADS // B_KV_HEADS
B_WIDTH = B_HEADS * HEAD_DIM
KV_WIDTH = B_KV_HEADS * HEAD_DIM
BLOCK = 128
REL_BUCKETS = 32
REL_MAX_DIST = 128
NORM_EPS = 1e-6
IN_WIDTH = 3 * A_WIDTH + 2 * B_WIDTH + 2 * KV_WIDTH
NEG_BIG = -1e30

UV_LO, AZ_LO, Q_LO, K_LO, V_LO, BZ_LO = 0, 2 * A_WIDTH, 3 * A_WIDTH, 3 * A_WIDTH + B_WIDTH, \
    3 * A_WIDTH + B_WIDTH + KV_WIDTH, 3 * A_WIDTH + B_WIDTH + 2 * KV_WIDTH

V7X_LANES = 128
V7X_MXU_COLS = 256
N_PAIRS = A_WIDTH // V7X_LANES
BLOCKS_PER_STEP = 8
TOKENS_PER_STEP = BLOCKS_PER_STEP * BLOCK


def _vmem_limit_bytes():
    f32_bytes, bf16_bytes, buffers = 4, 2, 2
    windows = buffers * (
        2 * TOKENS_PER_STEP * D_MODEL * f32_bytes
        + BLOCK * D_MODEL * f32_bytes
        + D_MODEL * IN_WIDTH * bf16_bytes + D_MODEL * D_MODEL * bf16_bytes
        + (BLOCK * A_WIDTH + BLOCK * BLOCK) * f32_bytes + N_PAIRS * BLOCK * 2 * BLOCK * f32_bytes)
    scratch = ((D_MODEL * IN_WIDTH + D_MODEL * D_MODEL) * bf16_bytes
               + 2 * B_HEADS * BLOCK * BLOCK * f32_bytes
               + 3 * BLOCK * A_WIDTH * f32_bytes
               + (N_PAIRS * 2 + B_WIDTH // BLOCK + 4 + 4 + D_MODEL // BLOCK + N_PAIRS * 2)
               * BLOCK * BLOCK * bf16_bytes)
    temporaries = 2 * BLOCK * IN_WIDTH * f32_bytes * BLOCKS_PER_STEP // 2
    return windows + scratch + temporaries


def _combined_bucket_map():
    i = jnp.arange(BLOCK)[:, None]
    j = jnp.arange(BLOCK)[None, :]
    dist = jnp.where(j <= i, i - j, i + BLOCK - j)
    max_exact = REL_BUCKETS // 2
    safe = jnp.maximum(dist, 1).astype(jnp.float32)
    large = max_exact + (jnp.log(safe / max_exact) / math.log(REL_MAX_DIST / max_exact)
                         * (REL_BUCKETS - max_exact)).astype(jnp.int32)
    large = jnp.minimum(large, REL_BUCKETS - 1)
    return jnp.where(dist < max_exact, dist, large).astype(jnp.int32)


def _gelu_tanh(x):
    k1 = -2.0 * math.sqrt(2.0 / math.pi)
    return x / (1.0 + jnp.exp(x * (k1 + (k1 * 0.044715) * (x * x))))


def _layer_kernel(relb_ref, sink_ref,
                  x_ref, xnext_ref, preg_ref, win_ref, lng_ref, lnb_ref, wsp_ref, bsx_ref, bucket_ref,
                  wout_ref, postg_ref,
                  o_ref,
                  kz0_ref, kz1_ref, va0_ref, va1_ref, wcat_ref, bias_ref,
                  pend_u_ref, pend_vn_ref, pend_ga_ref, pend_gb_ref, pend_q_ref, pend_kv_ref,
                  h_ref, win_s_ref, wout_s_ref):
    f32, bf16 = jnp.float32, jnp.bfloat16
    t = pl.program_id(1)
    first_step = (pl.program_id(0) == 0) & (t == 0)

    row = lax.broadcasted_iota(jnp.int32, (BLOCK, BLOCK), 0)
    col = lax.broadcasted_iota(jnp.int32, (BLOCK, BLOCK), 1)
    tril = col <= row
    low_half = col < HEAD_DIM

    @pl.when(first_step)
    def _init():
        half_packed = BLOCK // 2
        low_lanes = lax.broadcasted_iota(jnp.int32, (half_packed, BLOCK), 1) < HEAD_DIM

        def pair_head_columns(chunk):
            words = pltpu.bitcast(chunk, jnp.uint32)
            old = [words[:, s * BLOCK:(s + 1) * BLOCK] for s in range(N_PAIRS)]
            swapped = [pltpu.roll(slab, HEAD_DIM, axis=1) for slab in old]
            new = []
            for g in range(B_GROUP):
                src0, src1 = g // 2, B_GROUP // 2 + g // 2
                if g % 2 == 0:
                    new.append(jnp.where(low_lanes, old[src0], swapped[src1]))
                else:
                    new.append(jnp.where(low_lanes, swapped[src0], old[src1]))
            return pltpu.bitcast(jnp.concatenate(new, axis=1), bf16)

        for r in range(0, D_MODEL, BLOCK):
            rows = slice(r, r + BLOCK)
            win_s_ref[rows, :Q_LO] = win_ref[rows, :Q_LO]
            win_s_ref[rows, Q_LO:K_LO] = pair_head_columns(win_ref[rows, Q_LO:K_LO])
            win_s_ref[rows, K_LO:BZ_LO] = win_ref[rows, K_LO:BZ_LO]
            win_s_ref[rows, BZ_LO:] = pair_head_columns(win_ref[rows, BZ_LO:])
        wout_s_ref[:A_WIDTH, :] = wout_ref[:A_WIDTH, :]
        for g in range(B_GROUP):
            for kv in range(B_KV_HEADS):
                dst = A_WIDTH + (g * B_KV_HEADS + kv) * HEAD_DIM
                src = A_WIDTH + (kv * B_GROUP + g) * HEAD_DIM
                wout_s_ref[dst:dst + HEAD_DIM, :] = wout_ref[src:src + HEAD_DIM, :]
        wcol = lax.broadcasted_iota(jnp.int32, (BLOCK, 2 * BLOCK), 1)
        wrow = lax.broadcasted_iota(jnp.int32, (BLOCK, 2 * BLOCK), 0)
        wmask = (wcol % BLOCK) <= wrow
        for p in range(N_PAIRS):
            wcat_ref[p] = jnp.where(wmask, wsp_ref[p], 0.0).astype(bf16)
        bucket = bucket_ref[...]
        for h in range(B_HEADS):
            bias_ref[0, h] = jnp.zeros((BLOCK, BLOCK), f32)

        def fill(b, carry):
            hit = bucket == b
            for h in range(B_HEADS):
                bias_ref[0, h] = jnp.where(hit, relb_ref[b, h], bias_ref[0, h])
            return carry
        lax.fori_loop(0, REL_BUCKETS, fill, 0)
        for h in range(B_HEADS):
            bias_ref[1, h] = jnp.where(tril, bias_ref[0, h], NEG_BIG)

    carry_refs = (kz0_ref, kz1_ref, va0_ref, va1_ref)

    @pl.when(t == 0)
    def _reset_carry():
        for ref in carry_refs:
            ref[...] = jnp.zeros((BLOCK, BLOCK), bf16)

    first_flag = (t == 0).astype(jnp.int32)
    prev = tuple(ref[...] for ref in carry_refs)
    ones_cols = jnp.ones((2 * BLOCK, BLOCK), bf16)

    def pre_norm(xb):
        ms = jnp.mean(xb * xb, axis=-1, keepdims=True)
        h_ref[...] = (xb * lax.rsqrt(ms + NORM_EPS) * preg_ref[...]).astype(bf16)
        return {}

    def in_uv(st):
        st["uv"] = proj(UV_LO, 2 * A_WIDTH)

    def proj(lo, width):
        tiles = [jnp.dot(h_ref[...], win_s_ref[:, c:c + V7X_MXU_COLS], preferred_element_type=f32)
                 for c in range(lo, lo + width, V7X_MXU_COLS)]
        return tiles[0] if len(tiles) == 1 else jnp.concatenate(tiles, axis=1)

    def in_gates(st):
        st["gate_a"] = jax.nn.silu(proj(AZ_LO, A_WIDTH))
        st["gate_b"] = jax.nn.silu(proj(BZ_LO, B_WIDTH))

    def in_qkv(st):
        st["q"] = (proj(Q_LO, B_WIDTH) * (HEAD_DIM ** -0.5)).astype(bf16)
        k_v = proj(K_LO, 2 * KV_WIDTH)
        k, vv = k_v[:, :KV_WIDTH], k_v[:, KV_WIDTH:]
        k, vv = k.astype(bf16), vv.astype(bf16)
        zero, one = jnp.zeros((), bf16), jnp.ones((), bf16)
        st["kv"] = (jnp.where(low_half, k, zero), jnp.where(low_half, zero, k),
                    jnp.where(low_half, vv, one), jnp.where(low_half, one, vv))

    def gelu_and_layer_norm(st):
        uv = _gelu_tanh(st.pop("uv"))
        st["u"], v = uv[:, :A_WIDTH], uv[:, A_WIDTH:]
        mu = jnp.mean(v, axis=-1, keepdims=True)
        vc = v - mu
        var = jnp.mean(vc * vc, axis=-1, keepdims=True)
        vn = vc * lax.rsqrt(var + NORM_EPS) * lng_ref[...] + lnb_ref[...]
        vn = vn.astype(bf16)
        zero = jnp.zeros((), bf16)
        st["vn_rhs"] = [
            jnp.concatenate([jnp.where(low_half, vn[:, p * BLOCK:(p + 1) * BLOCK], zero),
                             jnp.where(low_half, zero, vn[:, p * BLOCK:(p + 1) * BLOCK])],
                            axis=0) for p in range(N_PAIRS)]

    def scores_and_softmax(st, prev, flag):
        q = st.pop("q")
        q_stack = jnp.concatenate([q[:, g * BLOCK:(g + 1) * BLOCK] for g in range(B_GROUP)], axis=0)
        st["probs"], st["row_max"] = [], []
        for kv in range(B_KV_HEADS):
            keys = jnp.concatenate([prev[kv], st["kv"][kv]], axis=0)
            s = lax.dot_general(q_stack, keys, (((1,), (1,)), ((), ())),
                                preferred_element_type=f32)
            probs, row_max = [], []
            for g in range(B_GROUP):
                head = kv * B_GROUP + g
                sg = s[g * BLOCK:(g + 1) * BLOCK, :]
                logits = jnp.where(tril, sg[:, BLOCK:], sg[:, :BLOCK]) + bias_ref[flag, head]
                sink = sink_ref[head]
                m = jnp.maximum(jnp.max(logits, axis=-1, keepdims=True), sink)
                e = jnp.exp(logits - m)
                e = e.astype(bf16)
                zero = jnp.zeros((), bf16)
                probs.append(jnp.concatenate([jnp.where(tril, zero, e), jnp.where(tril, e, zero)], axis=1))
                row_max.append(m)
            st["probs"].append(jnp.concatenate(probs, axis=0))
            st["row_max"].append(row_max)

    def spatial_mix(st):
        mixed = [jnp.dot(wcat_ref[p], rhs, preferred_element_type=f32)
                 for p, rhs in enumerate(st.pop("vn_rhs"))]
        mixed = jnp.concatenate(mixed, axis=1) + bsx_ref[...]
        st["y_a"] = ((st.pop("u") * mixed) * st.pop("gate_a")).astype(bf16)

    def attend(st, prev):
        pv = []
        for kv in range(B_KV_HEADS):
            vals = jnp.concatenate([jnp.concatenate([prev[2 + kv], st["kv"][2 + kv]], axis=0), ones_cols],
                                   axis=1)
            pv.append(jnp.dot(st["probs"][kv], vals, preferred_element_type=f32))
        attn = []
        for g in range(B_GROUP):
            rows = slice(g * BLOCK, (g + 1) * BLOCK)
            num = jnp.where(low_half, pv[0][rows, :BLOCK], pv[1][rows, :BLOCK])
            row_sum = jnp.where(low_half, pv[0][rows, BLOCK:], pv[1][rows, BLOCK:])
            sink = jnp.where(low_half, sink_ref[g], sink_ref[B_GROUP + g])
            m = jnp.where(low_half, st["row_max"][0][g], st["row_max"][1][g])
            attn.append(num / (row_sum + jnp.exp(sink - m)))
        attn = jnp.concatenate(attn, axis=1)
        st["y_b"] = (attn * st.pop("gate_b")).astype(bf16)

    def out_norm_residual(st, blk):
        r0 = blk * BLOCK
        y = jnp.concatenate([st.pop("y_a"), st.pop("y_b")], axis=1)
        o = jnp.dot(y, wout_s_ref[...], preferred_element_type=f32)
        os_ = jnp.mean(o * o, axis=-1, keepdims=True)
        o_ref[0, r0:r0 + BLOCK, :] = (x_ref[0, r0:r0 + BLOCK, :]
                                      + o * lax.rsqrt(os_ + NORM_EPS) * postg_ref[...])

    def save_pending(st):
        pend_u_ref[...] = st["u"]
        for p in range(N_PAIRS):
            pend_vn_ref[p] = st["vn_rhs"][p]
        pend_ga_ref[...] = st["gate_a"]
        pend_gb_ref[...] = st["gate_b"]
        pend_q_ref[...] = st["q"]
        for i in range(2 * B_KV_HEADS):
            pend_kv_ref[i] = st["kv"][i]

    def load_pending():
        return {"u": pend_u_ref[...], "vn_rhs": [pend_vn_ref[p] for p in range(N_PAIRS)],
                "gate_a": pend_ga_ref[...], "gate_b": pend_gb_ref[...], "q": pend_q_ref[...],
                "kv": tuple(pend_kv_ref[i] for i in range(2 * B_KV_HEADS))}

    def input_stage(xb):
        st = pre_norm(xb)
        in_uv(st)
        in_qkv(st)
        in_gates(st)
        gelu_and_layer_norm(st)
        return st

    @pl.when(first_step)
    def _prologue():
        save_pending(input_stage(x_ref[0, 0:BLOCK, :]))

    pending = load_pending()
    for blk in range(BLOCKS_PER_STEP):
        nxt = (blk + 1) * BLOCK
        x_next = x_ref[0, nxt:nxt + BLOCK, :] if blk + 1 < BLOCKS_PER_STEP else xnext_ref[0]
        scores_and_softmax(pending, prev, first_flag if blk == 0 else 0)
        spatial_mix(pending)
        st = pre_norm(x_next)
        in_uv(st)
        in_qkv(st)
        last = blk + 1 == BLOCKS_PER_STEP
        if not last:
            in_gates(st)
        gelu_and_layer_norm(st)
        attend(pending, prev)
        out_norm_residual(pending, blk)
        if last:
            in_gates(st)
        prev = pending["kv"]
        pending = st
    save_pending(pending)

    for ref, val in zip(carry_refs, prev):
        ref[...] = val


def _layer(x, rel_bias, sinks, pre_g, w_in, ln_g, ln_b, w_sp, bs_x, bucket, w_out, post_g):
    bsz, seq, _ = x.shape
    steps = seq // TOKENS_PER_STEP
    const2 = lambda b, t: (0, 0)
    const3 = lambda b, t: (0, 0, 0)
    smem = pl.BlockSpec(memory_space=pltpu.SMEM)
    blocks_per_seq = seq // BLOCK
    last_block = bsz * blocks_per_seq - 1

    def next_block_map(b, t):
        n = jnp.minimum(b * blocks_per_seq + (t + 1) * BLOCKS_PER_STEP, last_block)
        return (n // blocks_per_seq, n % blocks_per_seq, 0)

    return pl.pallas_call(
        _layer_kernel,
        grid=(bsz, steps),
        in_specs=[
            smem, smem,
            pl.BlockSpec((1, TOKENS_PER_STEP, D_MODEL), lambda b, t: (b, t, 0)),
            pl.BlockSpec((1, BLOCK, D_MODEL), next_block_map),
            pl.BlockSpec((1, D_MODEL), const2),
            pl.BlockSpec((D_MODEL, IN_WIDTH), const2),
            pl.BlockSpec((1, A_WIDTH), const2),
            pl.BlockSpec((1, A_WIDTH), const2),
            pl.BlockSpec((N_PAIRS, BLOCK, 2 * BLOCK), const3),
            pl.BlockSpec((BLOCK, A_WIDTH), const2),
            pl.BlockSpec((BLOCK, BLOCK), const2),
            pl.BlockSpec((D_MODEL, D_MODEL), const2),
            pl.BlockSpec((1, D_MODEL), const2),
        ],
        out_specs=pl.BlockSpec((1, TOKENS_PER_STEP, D_MODEL), lambda b, t: (b, t, 0)),
        out_shape=jax.ShapeDtypeStruct(x.shape, x.dtype),
        scratch_shapes=[
            pltpu.VMEM((BLOCK, BLOCK), jnp.bfloat16),
            pltpu.VMEM((BLOCK, BLOCK), jnp.bfloat16),
            pltpu.VMEM((BLOCK, BLOCK), jnp.bfloat16),
            pltpu.VMEM((BLOCK, BLOCK), jnp.bfloat16),
            pltpu.VMEM((N_PAIRS, BLOCK, 2 * BLOCK), jnp.bfloat16),
            pltpu.VMEM((2, B_HEADS, BLOCK, BLOCK), jnp.float32),
            pltpu.VMEM((BLOCK, A_WIDTH), jnp.float32),
            pltpu.VMEM((N_PAIRS, 2 * BLOCK, BLOCK), jnp.bfloat16),
            pltpu.VMEM((BLOCK, A_WIDTH), jnp.float32),
            pltpu.VMEM((BLOCK, B_WIDTH), jnp.float32),
            pltpu.VMEM((BLOCK, B_WIDTH), jnp.bfloat16),
            pltpu.VMEM((2 * B_KV_HEADS, BLOCK, BLOCK), jnp.bfloat16),
            pltpu.VMEM((BLOCK, D_MODEL), jnp.bfloat16),
            pltpu.VMEM((D_MODEL, IN_WIDTH), jnp.bfloat16),
            pltpu.VMEM((D_MODEL, D_MODEL), jnp.bfloat16),
        ],
        compiler_params=pltpu.CompilerParams(
            dimension_semantics=("arbitrary", "arbitrary"),
            vmem_limit_bytes=_vmem_limit_bytes()),
        name="hybrid_layer",
    )(rel_bias, sinks, x, x, pre_g, w_in, ln_g, ln_b, w_sp, bs_x, bucket, w_out, post_g)


@jax.jit
def kernel(x, pre_norm_g, w_in, ln_v_g, ln_v_b, w_spatial, b_spatial, sinks, rel_bias, w_out, post_norm_g):
    depth = w_in.shape[0]
    w_in_p = w_in.astype(jnp.bfloat16)
    w_out_p = w_out.astype(jnp.bfloat16)
    w_sp = w_spatial.reshape(depth, N_PAIRS, 2, BLOCK, BLOCK).transpose(0, 1, 3, 2, 4)
    w_sp = w_sp.reshape(depth, N_PAIRS, BLOCK, 2 * BLOCK)
    bs_x = jnp.repeat(jnp.swapaxes(b_spatial, 1, 2), A_GROUP_DIM, axis=2)
    bucket = _combined_bucket_map()
    for layer in range(depth):
        x = _layer(x, rel_bias, sinks[layer], pre_norm_g[layer][None], w_in_p[layer],
                   ln_v_g[layer][None], ln_v_b[layer][None], w_sp[layer], bs_x[layer], bucket,
                   w_out_p[layer], post_norm_g[layer][None])
    return x
```

```python
import math

import jax
import jax.numpy as jnp
from jax import lax
from jax.experimental import pallas as pl
from jax.experimental.pallas import tpu as pltpu

D_MODEL = 1024
A_WIDTH = 512
A_GROUPS = 8
A_GROUP_DIM = A_WIDTH // A_GROUPS
B_HEADS = 8
HEAD_DIM = 64
B_KV_HEADS = 2
B_GROUP = B_HEADS // B_KV_HEADS
B_WIDTH = B_HEADS * HEAD_DIM
KV_WIDTH = B_KV_HEADS * HEAD_DIM
BLOCK = 128
REL_BUCKETS = 32
REL_MAX_DIST = 128
NORM_EPS = 1e-6
IN_WIDTH = 3 * A_WIDTH + 2 * B_WIDTH + 2 * KV_WIDTH
NEG_BIG = -1e30

UV_LO, AZ_LO, Q_LO, K_LO, V_LO, BZ_LO = 0, 2 * A_WIDTH, 3 * A_WIDTH, 3 * A_WIDTH + B_WIDTH, \
    3 * A_WIDTH + B_WIDTH + KV_WIDTH, 3 * A_WIDTH + B_WIDTH + 2 * KV_WIDTH

V7X_LANES = 128
V7X_MXU_COLS = 256
N_PAIRS = A_WIDTH // V7X_LANES
BLOCKS_PER_STEP = 8
TOKENS_PER_STEP = BLOCKS_PER_STEP * BLOCK


def _vmem_limit_bytes():
    f32_bytes, bf16_bytes, buffers = 4, 2, 2
    windows = buffers * (
        2 * TOKENS_PER_STEP * D_MODEL * f32_bytes
        + BLOCK * D_MODEL * f32_bytes
        + D_MODEL * IN_WIDTH * bf16_bytes + D_MODEL * D_MODEL * bf16_bytes
        + (BLOCK * A_WIDTH + BLOCK * BLOCK) * f32_bytes + A_GROUPS * BLOCK * BLOCK * f32_bytes)
    scratch = ((D_MODEL * IN_WIDTH + D_MODEL * D_MODEL) * bf16_bytes
               + 2 * B_HEADS * BLOCK * BLOCK * f32_bytes
               + 4 * BLOCK * A_WIDTH * f32_bytes
               + (N_PAIRS * 2 + B_WIDTH // BLOCK + 4 + 4 + 2 * D_MODEL // BLOCK)
               * BLOCK * BLOCK * bf16_bytes)
    temporaries = 2 * BLOCK * IN_WIDTH * f32_bytes * BLOCKS_PER_STEP // 2
    return windows + scratch + temporaries


def _combined_bucket_map():
    i = jnp.arange(BLOCK)[:, None]
    j = jnp.arange(BLOCK)[None, :]
    dist = jnp.where(j <= i, i - j, i + BLOCK - j)
    max_exact = REL_BUCKETS // 2
    safe = jnp.maximum(dist, 1).astype(jnp.float32)
    large = max_exact + (jnp.log(safe / max_exact) / math.log(REL_MAX_DIST / max_exact)
                         * (REL_BUCKETS - max_exact)).astype(jnp.int32)
    large = jnp.minimum(large, REL_BUCKETS - 1)
    return jnp.where(dist < max_exact, dist, large).astype(jnp.int32)


def _gelu_tanh(x):
    k1 = -2.0 * math.sqrt(2.0 / math.pi)
    return x / (1.0 + jnp.exp(x * (k1 + (k1 * 0.044715) * (x * x))))


def _layer_kernel(relb_ref, sink_ref,
                  x_ref, xnext_ref, preg_ref, win_ref, lng_ref, lnb_ref, wsp_ref, bsx_ref, bucket_ref,
                  wout_ref, postg_ref,
                  o_ref,
                  kz0_ref, kz1_ref, va0_ref, va1_ref, wcat_ref, bias_ref,
                  pend_u_ref, pend_mixed_ref, pend_ga_ref, pend_gb_ref, pend_q_ref, pend_kv_ref,
                  h_ref, win_s_ref, wout_s_ref):
    f32, bf16 = jnp.float32, jnp.bfloat16
    t = pl.program_id(1)
    first_step = (pl.program_id(0) == 0) & (t == 0)

    row = lax.broadcasted_iota(jnp.int32, (BLOCK, BLOCK), 0)
    col = lax.broadcasted_iota(jnp.int32, (BLOCK, BLOCK), 1)
    tril = col <= row
    low_half = col < HEAD_DIM

    @pl.when(first_step)
    def _init():
        half_packed = BLOCK // 2
        low_lanes = lax.broadcasted_iota(jnp.int32, (half_packed, BLOCK), 1) < HEAD_DIM

        def pair_head_columns(chunk):
            words = pltpu.bitcast(chunk, jnp.uint32)
            old = [words[:, s * BLOCK:(s + 1) * BLOCK] for s in range(N_PAIRS)]
            swapped = [pltpu.roll(slab, HEAD_DIM, axis=1) for slab in old]
            new = []
            for g in range(B_GROUP):
                src0, src1 = g // 2, B_GROUP // 2 + g // 2
                if g % 2 == 0:
                    new.append(jnp.where(low_lanes, old[src0], swapped[src1]))
                else:
                    new.append(jnp.where(low_lanes, swapped[src0], old[src1]))
            return pltpu.bitcast(jnp.concatenate(new, axis=1), bf16)

        for r in range(0, D_MODEL, BLOCK):
            rows = slice(r, r + BLOCK)
            win_s_ref[rows, :Q_LO] = win_ref[rows, :Q_LO]
            win_s_ref[rows, Q_LO:K_LO] = pair_head_columns(win_ref[rows, Q_LO:K_LO])
            win_s_ref[rows, K_LO:BZ_LO] = win_ref[rows, K_LO:BZ_LO]
            win_s_ref[rows, BZ_LO:] = pair_head_columns(win_ref[rows, BZ_LO:])
        wout_s_ref[:A_WIDTH, :] = wout_ref[:A_WIDTH, :]
        for g in range(B_GROUP):
            for kv in range(B_KV_HEADS):
                dst = A_WIDTH + (g * B_KV_HEADS + kv) * HEAD_DIM
                src = A_WIDTH + (kv * B_GROUP + g) * HEAD_DIM
                wout_s_ref[dst:dst + HEAD_DIM, :] = wout_ref[src:src + HEAD_DIM, :]
        for p in range(N_PAIRS):
            wcat_ref[p] = jnp.concatenate(
                [jnp.where(tril, wsp_ref[2 * p + i], 0.0).astype(bf16) for i in range(2)], axis=1)
        bucket = bucket_ref[...]
        for h in range(B_HEADS):
            bias_ref[0, h] = jnp.zeros((BLOCK, BLOCK), f32)

        def fill(b, carry):
            hit = bucket == b
            for h in range(B_HEADS):
                bias_ref[0, h] = jnp.where(hit, relb_ref[b, h], bias_ref[0, h])
            return carry
        lax.fori_loop(0, REL_BUCKETS, fill, 0)
        for h in range(B_HEADS):
            bias_ref[1, h] = jnp.where(tril, bias_ref[0, h], NEG_BIG)

    carry_refs = (kz0_ref, kz1_ref, va0_ref, va1_ref)

    @pl.when(t == 0)
    def _reset_carry():
        for ref in carry_refs:
            ref[...] = jnp.zeros((BLOCK, BLOCK), bf16)

    first_flag = (t == 0).astype(jnp.int32)
    prev = tuple(ref[...] for ref in carry_refs)
    ones_cols = jnp.ones((2 * BLOCK, BLOCK), bf16)

    def lane_mask(rows):
        return lax.broadcasted_iota(jnp.int32, (rows, BLOCK), 1) < HEAD_DIM

    def split_rows(a, n):
        return [a[i * BLOCK:(i + 1) * BLOCK] for i in range(n)]

    def pre_norm(xs):
        for i, xb in enumerate(xs):
            ms = jnp.mean(xb * xb, axis=-1, keepdims=True)
            h_ref[i * BLOCK:(i + 1) * BLOCK, :] = (
                xb * lax.rsqrt(ms + NORM_EPS) * preg_ref[...]).astype(bf16)

    def proj(n, lo, width):
        tiles = [jnp.dot(h_ref[0:n * BLOCK, :], win_s_ref[:, c:c + V7X_MXU_COLS],
                         preferred_element_type=f32)
                 for c in range(lo, lo + width, V7X_MXU_COLS)]
        return tiles[0] if len(tiles) == 1 else jnp.concatenate(tiles, axis=1)

    def in_uv_qkv(n):
        uv = proj(n, UV_LO, 2 * A_WIDTH)
        q = (proj(n, Q_LO, B_WIDTH) * (HEAD_DIM ** -0.5)).astype(bf16)
        k_v = proj(n, K_LO, 2 * KV_WIDTH)
        k, vv = k_v[:, :KV_WIDTH].astype(bf16), k_v[:, KV_WIDTH:].astype(bf16)
        low = lane_mask(n * BLOCK)
        zero, one = jnp.zeros((), bf16), jnp.ones((), bf16)
        kv = [split_rows(a, n) for a in (jnp.where(low, k, zero), jnp.where(low, zero, k),
                                         jnp.where(low, vv, one), jnp.where(low, one, vv))]
        sts = [{"q": qi, "kv": tuple(a[i] for a in kv)} for i, qi in enumerate(split_rows(q, n))]
        return sts, uv

    def gelu_and_layer_norm(sts, uv):
        n = len(sts)
        uv = _gelu_tanh(uv)
        u, v = uv[:, :A_WIDTH], uv[:, A_WIDTH:]
        mu = jnp.mean(v, axis=-1, keepdims=True)
        vc = v - mu
        var = jnp.mean(vc * vc, axis=-1, keepdims=True)
        vn = (vc * lax.rsqrt(var + NORM_EPS) * lng_ref[...] + lnb_ref[...]).astype(bf16)
        for st, ui in zip(sts, split_rows(u, n)):
            st["u"] = ui
        return vn

    def spatial_mix(sts, vn):
        n = len(sts)
        zero = jnp.zeros((), bf16)
        mixed = [[] for _ in range(n)]
        for p in range(N_PAIRS):
            cols = []
            for slab in split_rows(vn[:, p * BLOCK:(p + 1) * BLOCK], n):
                cols.append(jnp.concatenate([jnp.where(low_half, slab, zero),
                                             jnp.where(low_half, zero, slab)], axis=0))
            rhs = cols[0] if n == 1 else jnp.concatenate(cols, axis=1)
            out = jnp.dot(wcat_ref[p], rhs, preferred_element_type=f32)
            for i in range(n):
                mixed[i].append(out[:, i * BLOCK:(i + 1) * BLOCK])
        for st, m in zip(sts, mixed):
            st["mixed"] = jnp.concatenate(m, axis=1) + bsx_ref[...]

    def in_gates(sts):
        n = len(sts)
        gate_a = jax.nn.silu(proj(n, AZ_LO, A_WIDTH))
        gate_b = jax.nn.silu(proj(n, BZ_LO, B_WIDTH))
        for st, ga, gb in zip(sts, split_rows(gate_a, n), split_rows(gate_b, n)):
            st["gate_a"], st["gate_b"] = ga, gb

    def scores_and_softmax(st, prev, flag):
        q = st.pop("q")
        q_stack = jnp.concatenate([q[:, g * BLOCK:(g + 1) * BLOCK] for g in range(B_GROUP)], axis=0)
        st["probs"], st["row_max"] = [], []
        for kv in range(B_KV_HEADS):
            keys = jnp.concatenate([prev[kv], st["kv"][kv]], axis=0)
            s = lax.dot_general(q_stack, keys, (((1,), (1,)), ((), ())),
                                preferred_element_type=f32)
            probs, row_max = [], []
            for g in range(B_GROUP):
                head = kv * B_GROUP + g
                sg = s[g * BLOCK:(g + 1) * BLOCK, :]
                logits = jnp.where(tril, sg[:, BLOCK:], sg[:, :BLOCK]) + bias_ref[flag, head]
                sink = sink_ref[head]
                m = jnp.maximum(jnp.max(logits, axis=-1, keepdims=True), sink)
                e = jnp.exp(logits - m)
                e = e.astype(bf16)
                zero = jnp.zeros((), bf16)
                probs.append(jnp.concatenate([jnp.where(tril, zero, e), jnp.where(tril, e, zero)], axis=1))
                row_max.append(m)
            st["probs"].append(jnp.concatenate(probs, axis=0))
            st["row_max"].append(row_max)

    def attend(st, prev):
        pv = []
        for kv in range(B_KV_HEADS):
            vals = jnp.concatenate([jnp.concatenate([prev[2 + kv], st["kv"][2 + kv]], axis=0), ones_cols],
                                   axis=1)
            pv.append(jnp.dot(st["probs"][kv], vals, preferred_element_type=f32))
        attn = []
        for g in range(B_GROUP):
            rows = slice(g * BLOCK, (g + 1) * BLOCK)
            num = jnp.where(low_half, pv[0][rows, :BLOCK], pv[1][rows, :BLOCK])
            row_sum = jnp.where(low_half, pv[0][rows, BLOCK:], pv[1][rows, BLOCK:])
            sink = jnp.where(low_half, sink_ref[g], sink_ref[B_GROUP + g])
            m = jnp.where(low_half, st["row_max"][0][g], st["row_max"][1][g])
            attn.append(num / (row_sum + jnp.exp(sink - m)))
        attn = jnp.concatenate(attn, axis=1)
        st["y_b"] = (attn * st.pop("gate_b")).astype(bf16)

    def out_norm_residual(st, blk):
        r0 = blk * BLOCK
        y_a = ((st.pop("u") * st.pop("mixed")) * st.pop("gate_a")).astype(bf16)
        y = jnp.concatenate([y_a, st.pop("y_b")], axis=1)
        o = jnp.dot(y, wout_s_ref[...], preferred_element_type=f32)
        os_ = jnp.mean(o * o, axis=-1, keepdims=True)
        o_ref[0, r0:r0 + BLOCK, :] = (x_ref[0, r0:r0 + BLOCK, :]
                                      + o * lax.rsqrt(os_ + NORM_EPS) * postg_ref[...])

    def save_pending(st):
        pend_u_ref[...] = st["u"]
        pend_mixed_ref[...] = st["mixed"]
        pend_ga_ref[...] = st["gate_a"]
        pend_gb_ref[...] = st["gate_b"]
        pend_q_ref[...] = st["q"]
        for i in range(2 * B_KV_HEADS):
            pend_kv_ref[i] = st["kv"][i]

    def load_pending():
        return {"u": pend_u_ref[...], "mixed": pend_mixed_ref[...],
                "gate_a": pend_ga_ref[...], "gate_b": pend_gb_ref[...], "q": pend_q_ref[...],
                "kv": tuple(pend_kv_ref[i] for i in range(2 * B_KV_HEADS))}

    @pl.when(first_step)
    def _prologue():
        pre_norm([x_ref[0, 0:BLOCK, :]])
        sts, uv = in_uv_qkv(1)
        spatial_mix(sts, gelu_and_layer_norm(sts, uv))
        in_gates(sts)
        save_pending(sts[0])

    def block_rows(blk):
        if blk < BLOCKS_PER_STEP:
            return x_ref[0, blk * BLOCK:(blk + 1) * BLOCK, :]
        return xnext_ref[0]

    pending = load_pending()
    for blk in range(0, BLOCKS_PER_STEP, 2):
        scores_and_softmax(pending, prev, first_flag if blk == 0 else 0)
        pre_norm([block_rows(blk + 1), block_rows(blk + 2)])
        ahead, uv = in_uv_qkv(2)
        vn = gelu_and_layer_norm(ahead, uv)
        attend(pending, prev)
        out_norm_residual(pending, blk)
        prev = pending["kv"]
        pending = ahead[0]
        scores_and_softmax(pending, prev, 0)
        spatial_mix(ahead, vn)
        in_gates(ahead)
        attend(pending, prev)
        out_norm_residual(pending, blk + 1)
        prev = pending["kv"]
        pending = ahead[1]
    save_pending(pending)

    for ref, val in zip(carry_refs, prev):
        ref[...] = val


def _layer(x, rel_bias, sinks, pre_g, w_in, ln_g, ln_b, w_sp, bs_x, bucket, w_out, post_g):
    bsz, seq, _ = x.shape
    steps = seq // TOKENS_PER_STEP
    const2 = lambda b, t: (0, 0)
    const3 = lambda b, t: (0, 0, 0)
    smem = pl.BlockSpec(memory_space=pltpu.SMEM)
    blocks_per_seq = seq // BLOCK
    last_block = bsz * blocks_per_seq - 1

    def next_block_map(b, t):
        n = jnp.minimum(b * blocks_per_seq + (t + 1) * BLOCKS_PER_STEP, last_block)
        return (n // blocks_per_seq, n % blocks_per_seq, 0)

    return pl.pallas_call(
        _layer_kernel,
        grid=(bsz, steps),
        in_specs=[
            smem, smem,
            pl.BlockSpec((1, TOKENS_PER_STEP, D_MODEL), lambda b, t: (b, t, 0)),
            pl.BlockSpec((1, BLOCK, D_MODEL), next_block_map),
            pl.BlockSpec((1, D_MODEL), const2),
            pl.BlockSpec((D_MODEL, IN_WIDTH), const2),
            pl.BlockSpec((1, A_WIDTH), const2),
            pl.BlockSpec((1, A_WIDTH), const2),
            pl.BlockSpec((A_GROUPS, BLOCK, BLOCK), const3),
            pl.BlockSpec((BLOCK, A_WIDTH), const2),
            pl.BlockSpec((BLOCK, BLOCK), const2),
            pl.BlockSpec((D_MODEL, D_MODEL), const2),
            pl.BlockSpec((1, D_MODEL), const2),
        ],
        out_specs=pl.BlockSpec((1, TOKENS_PER_STEP, D_MODEL), lambda b, t: (b, t, 0)),
        out_shape=jax.ShapeDtypeStruct(x.shape, x.dtype),
        scratch_shapes=[
            pltpu.VMEM((BLOCK, BLOCK), jnp.bfloat16),
            pltpu.VMEM((BLOCK, BLOCK), jnp.bfloat16),
            pltpu.VMEM((BLOCK, BLOCK), jnp.bfloat16),
            pltpu.VMEM((BLOCK, BLOCK), jnp.bfloat16),
            pltpu.VMEM((N_PAIRS, BLOCK, 2 * BLOCK), jnp.bfloat16),
            pltpu.VMEM((2, B_HEADS, BLOCK, BLOCK), jnp.float32),
            pltpu.VMEM((BLOCK, A_WIDTH), jnp.float32),
            pltpu.VMEM((BLOCK, A_WIDTH), jnp.float32),
            pltpu.VMEM((BLOCK, A_WIDTH), jnp.float32),
            pltpu.VMEM((BLOCK, B_WIDTH), jnp.float32),
            pltpu.VMEM((BLOCK, B_WIDTH), jnp.bfloat16),
            pltpu.VMEM((2 * B_KV_HEADS, BLOCK, BLOCK), jnp.bfloat16),
            pltpu.VMEM((2 * BLOCK, D_MODEL), jnp.bfloat16),
            pltpu.VMEM((D_MODEL, IN_WIDTH), jnp.bfloat16),
            pltpu.VMEM((D_MODEL, D_MODEL), jnp.bfloat16),
        ],
        compiler_params=pltpu.CompilerParams(
            dimension_semantics=("arbitrary", "arbitrary"),
            vmem_limit_bytes=_vmem_limit_bytes()),
        name="hybrid_layer",
    )(rel_bias, sinks, x, x, pre_g, w_in, ln_g, ln_b, w_sp, bs_x, bucket, w_out, post_g)


@jax.jit
def kernel(x, pre_norm_g, w_in, ln_v_g, ln_v_b, w_spatial, b_spatial, sinks, rel_bias, w_out, post_norm_g):
    depth = w_in.shape[0]
    w_in_p = w_in.astype(jnp.bfloat16)
    w_out_p = w_out.astype(jnp.bfloat16)
    bs_x = jnp.repeat(jnp.swapaxes(b_spatial, 1, 2), A_GROUP_DIM, axis=2)
    bucket = _combined_bucket_map()
    for layer in range(depth):
        x = _layer(x, rel_bias, sinks[layer], pre_norm_g[layer][None], w_in_p[layer],
                   ln_v_g[layer][None], ln_v_b[layer][None], w_spatial[layer], bs_x[layer], bucket,
                   w_out_p[layer], post_norm_g[layer][None])
    return x
```

```python
import math

import jax
import jax.numpy as jnp
from jax import lax
from jax.experimental import pallas as pl
from jax.experimental.pallas import tpu as pltpu

D_MODEL = 1024
A_WIDTH = 512
A_GROUPS = 8
A_GROUP_DIM = A_WIDTH // A_GROUPS
B_HEADS = 8
HEAD_DIM = 64
B_KV_HEADS = 2
B_GROUP = B_HEADS // B_KV_HEADS
B_WIDTH = B_HEADS * HEAD_DIM
KV_WIDTH = B_KV_HEADS * HEAD_DIM
BLOCK = 128
REL_BUCKETS = 32
REL_MAX_DIST = 128
NORM_EPS = 1e-6
IN_WIDTH = 3 * A_WIDTH + 2 * B_WIDTH + 2 * KV_WIDTH
NEG_BIG = -1e30

UV_LO, AZ_LO, Q_LO, K_LO, V_LO, BZ_LO = 0, 2 * A_WIDTH, 3 * A_WIDTH, 3 * A_WIDTH + B_WIDTH, \
    3 * A_WIDTH + B_WIDTH + KV_WIDTH, 3 * A_WIDTH + B_WIDTH + 2 * KV_WIDTH

V7X_LANES = 128
V7X_MXU_COLS = 256
N_PAIRS = A_WIDTH // V7X_LANES
BLOCKS_PER_STEP = 8
TOKENS_PER_STEP = BLOCKS_PER_STEP * BLOCK


def _vmem_limit_bytes():
    f32_bytes, bf16_bytes, buffers = 4, 2, 2
    windows = buffers * (
        2 * TOKENS_PER_STEP * D_MODEL * f32_bytes
        + BLOCK * D_MODEL * f32_bytes
        + D_MODEL * IN_WIDTH * bf16_bytes + D_MODEL * D_MODEL * bf16_bytes
        + (BLOCK * A_WIDTH + BLOCK * BLOCK) * f32_bytes + A_GROUPS * BLOCK * BLOCK * f32_bytes)
    scratch = ((D_MODEL * IN_WIDTH + D_MODEL * D_MODEL) * bf16_bytes
               + 2 * B_HEADS * BLOCK * BLOCK * f32_bytes
               + 4 * BLOCK * A_WIDTH * f32_bytes
               + (N_PAIRS * 2 + B_WIDTH // BLOCK + 4 + 4 + 2 * D_MODEL // BLOCK)
               * BLOCK * BLOCK * bf16_bytes)
    temporaries = 2 * BLOCK * IN_WIDTH * f32_bytes * BLOCKS_PER_STEP // 2
    return windows + scratch + temporaries


def _combined_bucket_map():
    i = jnp.arange(BLOCK)[:, None]
    j = jnp.arange(BLOCK)[None, :]
    dist = jnp.where(j <= i, i - j, i + BLOCK - j)
    max_exact = REL_BUCKETS // 2
    safe = jnp.maximum(dist, 1).astype(jnp.float32)
    large = max_exact + (jnp.log(safe / max_exact) / math.log(REL_MAX_DIST / max_exact)
                         * (REL_BUCKETS - max_exact)).astype(jnp.int32)
    large = jnp.minimum(large, REL_BUCKETS - 1)
    return jnp.where(dist < max_exact, dist, large).astype(jnp.int32)


def _gelu_tanh(x):
    k1 = -2.0 * math.sqrt(2.0 / math.pi)
    return x / (1.0 + jnp.exp(x * (k1 + (k1 * 0.044715) * (x * x))))


def _layer_kernel(relb_ref, sink_ref,
                  x_ref, xnext_ref, preg_ref, win_ref, lng_ref, lnb_ref, wsp_ref, bsx_ref, bucket_ref,
                  wout_ref, postg_ref,
                  o_ref,
                  kz0_ref, kz1_ref, va0_ref, va1_ref, wcat_ref, bias_ref,
                  pend_u_ref, pend_mixed_ref, pend_ga_ref, pend_gb_ref, pend_q_ref, pend_kv_ref,
                  h_ref, win_s_ref, wout_s_ref):
    f32, bf16 = jnp.float32, jnp.bfloat16
    t = pl.program_id(1)
    first_step = (pl.program_id(0) == 0) & (t == 0)

    row = lax.broadcasted_iota(jnp.int32, (BLOCK, BLOCK), 0)
    col = lax.broadcasted_iota(jnp.int32, (BLOCK, BLOCK), 1)
    tril = col <= row
    low_half = col < HEAD_DIM

    @pl.when(first_step)
    def _init():
        half_packed = BLOCK // 2
        low_lanes = lax.broadcasted_iota(jnp.int32, (half_packed, BLOCK), 1) < HEAD_DIM

        def pair_head_columns(chunk):
            words = pltpu.bitcast(chunk, jnp.uint32)
            old = [words[:, s * BLOCK:(s + 1) * BLOCK] for s in range(N_PAIRS)]
            swapped = [pltpu.roll(slab, HEAD_DIM, axis=1) for slab in old]
            new = []
            for g in range(B_GROUP):
                src0, src1 = g // 2, B_GROUP // 2 + g // 2
                if g % 2 == 0:
                    new.append(jnp.where(low_lanes, old[src0], swapped[src1]))
                else:
                    new.append(jnp.where(low_lanes, swapped[src0], old[src1]))
            return pltpu.bitcast(jnp.concatenate(new, axis=1), bf16)

        for r in range(0, D_MODEL, BLOCK):
            rows = slice(r, r + BLOCK)
            win_s_ref[rows, :Q_LO] = win_ref[rows, :Q_LO]
            win_s_ref[rows, Q_LO:K_LO] = pair_head_columns(win_ref[rows, Q_LO:K_LO])
            win_s_ref[rows, K_LO:BZ_LO] = win_ref[rows, K_LO:BZ_LO]
            win_s_ref[rows, BZ_LO:] = pair_head_columns(win_ref[rows, BZ_LO:])
        wout_s_ref[:A_WIDTH, :] = wout_ref[:A_WIDTH, :]
        for g in range(B_GROUP):
            for kv in range(B_KV_HEADS):
                dst = A_WIDTH + (g * B_KV_HEADS + kv) * HEAD_DIM
                src = A_WIDTH + (kv * B_GROUP + g) * HEAD_DIM
                wout_s_ref[dst:dst + HEAD_DIM, :] = wout_ref[src:src + HEAD_DIM, :]
        for p in range(N_PAIRS):
            wcat_ref[p] = jnp.concatenate(
                [jnp.where(tril, wsp_ref[2 * p + i], 0.0).astype(bf16) for i in range(2)], axis=1)
        bucket = bucket_ref[...]
        for h in range(B_HEADS):
            bias_ref[0, h] = jnp.zeros((BLOCK, BLOCK), f32)

        def fill(b, carry):
            hit = bucket == b
            for h in range(B_HEADS):
                bias_ref[0, h] = jnp.where(hit, relb_ref[b, h], bias_ref[0, h])
            return carry
        lax.fori_loop(0, REL_BUCKETS, fill, 0)
        for h in range(B_HEADS):
            bias_ref[1, h] = jnp.where(tril, bias_ref[0, h], NEG_BIG)

    carry_refs = (kz0_ref, kz1_ref, va0_ref, va1_ref)

    @pl.when(t == 0)
    def _reset_carry():
        for ref in carry_refs:
            ref[...] = jnp.zeros((BLOCK, BLOCK), bf16)

    first_flag = (t == 0).astype(jnp.int32)
    prev = tuple(ref[...] for ref in carry_refs)
    ones_cols = jnp.ones((2 * BLOCK, BLOCK), bf16)

    def lane_mask(rows):
        return lax.broadcasted_iota(jnp.int32, (rows, BLOCK), 1) < HEAD_DIM

    def split_rows(a, n):
        return [a[i * BLOCK:(i + 1) * BLOCK] for i in range(n)]

    def pre_norm(xs):
        for i, xb in enumerate(xs):
            ms = jnp.mean(xb * xb, axis=-1, keepdims=True)
            h_ref[i * BLOCK:(i + 1) * BLOCK, :] = (
                xb * lax.rsqrt(ms + NORM_EPS) * preg_ref[...]).astype(bf16)

    def proj(n, lo, width):
        tiles = [jnp.dot(h_ref[0:n * BLOCK, :], win_s_ref[:, c:c + V7X_MXU_COLS],
                         preferred_element_type=f32)
                 for c in range(lo, lo + width, V7X_MXU_COLS)]
        return tiles[0] if len(tiles) == 1 else jnp.concatenate(tiles, axis=1)

    def in_uv_qkv(n):
        uv = proj(n, UV_LO, 2 * A_WIDTH)
        q = (proj(n, Q_LO, B_WIDTH) * (HEAD_DIM ** -0.5)).astype(bf16)
        k_v = proj(n, K_LO, 2 * KV_WIDTH)
        k, vv = k_v[:, :KV_WIDTH].astype(bf16), k_v[:, KV_WIDTH:].astype(bf16)
        low = lane_mask(n * BLOCK)
        zero, one = jnp.zeros((), bf16), jnp.ones((), bf16)
        kv = [split_rows(a, n) for a in (jnp.where(low, k, zero), jnp.where(low, zero, k),
                                         jnp.where(low, vv, one), jnp.where(low, one, vv))]
        sts = [{"q": qi, "kv": tuple(a[i] for a in kv)} for i, qi in enumerate(split_rows(q, n))]
        return sts, uv

    def gelu_and_layer_norm(sts, uv):
        n = len(sts)
        uv = _gelu_tanh(uv)
        u, v = uv[:, :A_WIDTH], uv[:, A_WIDTH:]
        mu = jnp.mean(v, axis=-1, keepdims=True)
        vc = v - mu
        var = jnp.mean(vc * vc, axis=-1, keepdims=True)
        vn = (vc * lax.rsqrt(var + NORM_EPS) * lng_ref[...] + lnb_ref[...]).astype(bf16)
        for st, ui in zip(sts, split_rows(u, n)):
            st["u"] = ui
        return vn

    def spatial_mix(sts, vn):
        n = len(sts)
        zero = jnp.zeros((), bf16)
        mixed = [[] for _ in range(n)]
        for p in range(N_PAIRS):
            cols = []
            for slab in split_rows(vn[:, p * BLOCK:(p + 1) * BLOCK], n):
                cols.append(jnp.concatenate([jnp.where(low_half, slab, zero),
                                             jnp.where(low_half, zero, slab)], axis=0))
            rhs = cols[0] if n == 1 else jnp.concatenate(cols, axis=1)
            out = jnp.dot(wcat_ref[p], rhs, preferred_element_type=f32)
            for i in range(n):
                mixed[i].append(out[:, i * BLOCK:(i + 1) * BLOCK])
        for st, m in zip(sts, mixed):
            st["mixed"] = jnp.concatenate(m, axis=1) + bsx_ref[...]

    def in_gates(sts):
        n = len(sts)
        gate_a = jax.nn.silu(proj(n, AZ_LO, A_WIDTH))
        gate_b = jax.nn.silu(proj(n, BZ_LO, B_WIDTH))
        for st, ga, gb in zip(sts, split_rows(gate_a, n), split_rows(gate_b, n)):
            st["gate_a"], st["gate_b"] = ga, gb

    def scores_and_softmax(st, prev, flag):
        q = st.pop("q")
        q_stack = jnp.concatenate([q[:, g * BLOCK:(g + 1) * BLOCK] for g in range(B_GROUP)], axis=0)
        st["probs"], st["row_max"] = [], []
        for kv in range(B_KV_HEADS):
            keys = jnp.concatenate([prev[kv], st["kv"][kv]], axis=0)
            s = lax.dot_general(q_stack, keys, (((1,), (1,)), ((), ())),
                                preferred_element_type=f32)
            probs, row_max = [], []
            for g in range(B_GROUP):
                head = kv * B_GROUP + g
                sg = s[g * BLOCK:(g + 1) * BLOCK, :]
                logits = jnp.where(tril, sg[:, BLOCK:], sg[:, :BLOCK]) + bias_ref[flag, head]
                sink = sink_ref[head]
                m = jnp.maximum(jnp.max(logits, axis=-1, keepdims=True), sink)
                e = jnp.exp(logits - m)
                e = e.astype(bf16)
                zero = jnp.zeros((), bf16)
                probs.append(jnp.concatenate([jnp.where(tril, zero, e), jnp.where(tril, e, zero)], axis=1))
                row_max.append(m)
            st["probs"].append(jnp.concatenate(probs, axis=0))
            st["row_max"].append(row_max)

    def attend(st, prev):
        pv = []
        for kv in range(B_KV_HEADS):
            vals = jnp.concatenate([jnp.concatenate([prev[2 + kv], st["kv"][2 + kv]], axis=0), ones_cols],
                                   axis=1)
            pv.append(jnp.dot(st["probs"][kv], vals, preferred_element_type=f32))
        attn = []
        for g in range(B_GROUP):
            rows = slice(g * BLOCK, (g + 1) * BLOCK)
            num = jnp.where(low_half, pv[0][rows, :BLOCK], pv[1][rows, :BLOCK])
            row_sum = jnp.where(low_half, pv[0][rows, BLOCK:], pv[1][rows, BLOCK:])
            sink = jnp.where(low_half, sink_ref[g], sink_ref[B_GROUP + g])
            m = jnp.where(low_half, st["row_max"][0][g], st["row_max"][1][g])
            attn.append(num / (row_sum + jnp.exp(sink - m)))
        attn = jnp.concatenate(attn, axis=1)
        st["y_b"] = (attn * st.pop("gate_b")).astype(bf16)

    def gate_mixed(st):
        y_a = ((st.pop("u") * st.pop("mixed")) * st.pop("gate_a")).astype(bf16)
        st["y"] = jnp.concatenate([y_a, st.pop("y_b")], axis=1)

    def out_norm_residual(sts, blk):
        n = len(sts)
        y = jnp.concatenate([st.pop("y") for st in sts], axis=0)
        o = jnp.dot(y, wout_s_ref[...], preferred_element_type=f32)
        os_ = jnp.mean(o * o, axis=-1, keepdims=True)
        rows = slice(blk * BLOCK, (blk + n) * BLOCK)
        o_ref[0, rows, :] = x_ref[0, rows, :] + o * lax.rsqrt(os_ + NORM_EPS) * postg_ref[...]

    def save_pending(st):
        pend_u_ref[...] = st["u"]
        pend_mixed_ref[...] = st["mixed"]
        pend_ga_ref[...] = st["gate_a"]
        pend_gb_ref[...] = st["gate_b"]
        pend_q_ref[...] = st["q"]
        for i in range(2 * B_KV_HEADS):
            pend_kv_ref[i] = st["kv"][i]

    def load_pending():
        return {"u": pend_u_ref[...], "mixed": pend_mixed_ref[...],
                "gate_a": pend_ga_ref[...], "gate_b": pend_gb_ref[...], "q": pend_q_ref[...],
                "kv": tuple(pend_kv_ref[i] for i in range(2 * B_KV_HEADS))}

    @pl.when(first_step)
    def _prologue():
        pre_norm([x_ref[0, 0:BLOCK, :]])
        sts, uv = in_uv_qkv(1)
        spatial_mix(sts, gelu_and_layer_norm(sts, uv))
        in_gates(sts)
        save_pending(sts[0])

    def block_rows(blk):
        if blk < BLOCKS_PER_STEP:
            return x_ref[0, blk * BLOCK:(blk + 1) * BLOCK, :]
        return xnext_ref[0]

    pending = load_pending()
    for blk in range(0, BLOCKS_PER_STEP, 2):
        scores_and_softmax(pending, prev, first_flag if blk == 0 else 0)
        pre_norm([block_rows(blk + 1), block_rows(blk + 2)])
        ahead, uv = in_uv_qkv(2)
        vn = gelu_and_layer_norm(ahead, uv)
        attend(pending, prev)
        gate_mixed(pending)
        out_norm_residual([pending] if blk == 0 else [held, pending], max(blk - 1, 0))
        prev = pending["kv"]
        pending = ahead[0]
        scores_and_softmax(pending, prev, 0)
        spatial_mix(ahead, vn)
        in_gates(ahead)
        attend(pending, prev)
        gate_mixed(pending)
        if blk + 2 == BLOCKS_PER_STEP:
            out_norm_residual([pending], blk + 1)
        held = pending
        prev = pending["kv"]
        pending = ahead[1]
    save_pending(pending)

    for ref, val in zip(carry_refs, prev):
        ref[...] = val


def _layer(layer, x, rel_bias, sinks, pre_g, w_in, ln_g, ln_b, w_sp, bs_x, bucket, w_out, post_g):
    bsz, seq, _ = x.shape
    steps = seq // TOKENS_PER_STEP
    const2 = lambda b, t: (0, 0)
    mat_of_layer = lambda b, t: (layer, 0, 0)
    stack_of_layer = lambda b, t: (layer, 0, 0, 0)
    smem = pl.BlockSpec(memory_space=pltpu.SMEM)
    blocks_per_seq = seq // BLOCK
    last_block = bsz * blocks_per_seq - 1

    def next_block_map(b, t):
        n = jnp.minimum(b * blocks_per_seq + (t + 1) * BLOCKS_PER_STEP, last_block)
        return (n // blocks_per_seq, n % blocks_per_seq, 0)

    return pl.pallas_call(
        _layer_kernel,
        grid=(bsz, steps),
        in_specs=[
            smem, smem,
            pl.BlockSpec((1, TOKENS_PER_STEP, D_MODEL), lambda b, t: (b, t, 0)),
            pl.BlockSpec((1, BLOCK, D_MODEL), next_block_map),
            pl.BlockSpec((None, 1, D_MODEL), mat_of_layer),
            pl.BlockSpec((None, D_MODEL, IN_WIDTH), mat_of_layer),
            pl.BlockSpec((None, 1, A_WIDTH), mat_of_layer),
            pl.BlockSpec((None, 1, A_WIDTH), mat_of_layer),
            pl.BlockSpec((None, A_GROUPS, BLOCK, BLOCK), stack_of_layer),
            pl.BlockSpec((None, BLOCK, A_WIDTH), mat_of_layer),
            pl.BlockSpec((BLOCK, BLOCK), const2),
            pl.BlockSpec((None, D_MODEL, D_MODEL), mat_of_layer),
            pl.BlockSpec((None, 1, D_MODEL), mat_of_layer),
        ],
        out_specs=pl.BlockSpec((1, TOKENS_PER_STEP, D_MODEL), lambda b, t: (b, t, 0)),
        out_shape=jax.ShapeDtypeStruct(x.shape, x.dtype),
        scratch_shapes=[
            pltpu.VMEM((BLOCK, BLOCK), jnp.bfloat16),
            pltpu.VMEM((BLOCK, BLOCK), jnp.bfloat16),
            pltpu.VMEM((BLOCK, BLOCK), jnp.bfloat16),
            pltpu.VMEM((BLOCK, BLOCK), jnp.bfloat16),
            pltpu.VMEM((N_PAIRS, BLOCK, 2 * BLOCK), jnp.bfloat16),
            pltpu.VMEM((2, B_HEADS, BLOCK, BLOCK), jnp.float32),
            pltpu.VMEM((BLOCK, A_WIDTH), jnp.float32),
            pltpu.VMEM((BLOCK, A_WIDTH), jnp.float32),
            pltpu.VMEM((BLOCK, A_WIDTH), jnp.float32),
            pltpu.VMEM((BLOCK, B_WIDTH), jnp.float32),
            pltpu.VMEM((BLOCK, B_WIDTH), jnp.bfloat16),
            pltpu.VMEM((2 * B_KV_HEADS, BLOCK, BLOCK), jnp.bfloat16),
            pltpu.VMEM((2 * BLOCK, D_MODEL), jnp.bfloat16),
            pltpu.VMEM((D_MODEL, IN_WIDTH), jnp.bfloat16),
            pltpu.VMEM((D_MODEL, D_MODEL), jnp.bfloat16),
        ],
        compiler_params=pltpu.CompilerParams(
            dimension_semantics=("arbitrary", "arbitrary"),
            vmem_limit_bytes=_vmem_limit_bytes()),
        name="hybrid_layer",
    )(rel_bias, sinks, x, x, pre_g, w_in, ln_g, ln_b, w_sp, bs_x, bucket, w_out, post_g)


@jax.jit
def kernel(x, pre_norm_g, w_in, ln_v_g, ln_v_b, w_spatial, b_spatial, sinks, rel_bias, w_out, post_norm_g):
    depth = w_in.shape[0]
    w_in_p = w_in.astype(jnp.bfloat16)
    w_out_p = w_out.astype(jnp.bfloat16)
    bs_x = jnp.repeat(jnp.swapaxes(b_spatial, 1, 2), A_GROUP_DIM, axis=2)
    bucket = _combined_bucket_map()
    for layer in range(depth):
        x = _layer(layer, x, rel_bias, sinks[layer], pre_norm_g[:, None], w_in_p, ln_v_g[:, None],
                   ln_v_b[:, None], w_spatial, bs_x, bucket, w_out_p, post_norm_g[:, None])
    return x
```

```python
import math

import jax
import jax.numpy as jnp
from jax import lax
from jax.experimental import pallas as pl
from jax.experimental.pallas import tpu as pltpu

D_MODEL = 1024
A_WIDTH = 512
A_GROUPS = 8
A_GROUP_DIM = A_WIDTH // A_GROUPS
B_HEADS = 8
HEAD_DIM = 64
B_KV_HEADS = 2
B_GROUP = B_HEADS // B_KV_HEADS
B_WIDTH = B_HEADS * HEAD_DIM
KV_WIDTH = B_KV_HEADS * HEAD_DIM
BLOCK = 128
REL_BUCKETS = 32
REL_MAX_DIST = 128
NORM_EPS = 1e-6
IN_WIDTH = 3 * A_WIDTH + 2 * B_WIDTH + 2 * KV_WIDTH
NEG_BIG = -1e30

UV_LO, AZ_LO, Q_LO, K_LO, V_LO, BZ_LO = 0, 2 * A_WIDTH, 3 * A_WIDTH, 3 * A_WIDTH + B_WIDTH, \
    3 * A_WIDTH + B_WIDTH + KV_WIDTH, 3 * A_WIDTH + B_WIDTH + 2 * KV_WIDTH

V7X_LANES = 128
V7X_MXU_COLS = 256
N_PAIRS = A_WIDTH // V7X_LANES
BLOCKS_PER_STEP = 8
TOKENS_PER_STEP = BLOCKS_PER_STEP * BLOCK


def _vmem_limit_bytes():
    f32_bytes, bf16_bytes, buffers = 4, 2, 2
    windows = buffers * (
        2 * TOKENS_PER_STEP * D_MODEL * f32_bytes
        + BLOCK * D_MODEL * f32_bytes
        + D_MODEL * IN_WIDTH * bf16_bytes + D_MODEL * D_MODEL * bf16_bytes
        + (BLOCK * A_WIDTH + BLOCK * BLOCK) * f32_bytes + A_GROUPS * BLOCK * BLOCK * f32_bytes)
    scratch = ((D_MODEL * IN_WIDTH + D_MODEL * D_MODEL) * bf16_bytes
               + 2 * B_HEADS * BLOCK * BLOCK * f32_bytes
               + 4 * BLOCK * A_WIDTH * f32_bytes
               + (N_PAIRS * 2 + B_WIDTH // BLOCK + 4 + 4 + 2 * D_MODEL // BLOCK)
               * BLOCK * BLOCK * bf16_bytes)
    temporaries = 2 * BLOCK * IN_WIDTH * f32_bytes * BLOCKS_PER_STEP // 2
    return windows + scratch + temporaries


def _combined_bucket_map():
    i = jnp.arange(BLOCK)[:, None]
    j = jnp.arange(BLOCK)[None, :]
    dist = jnp.where(j <= i, i - j, i + BLOCK - j)
    max_exact = REL_BUCKETS // 2
    safe = jnp.maximum(dist, 1).astype(jnp.float32)
    large = max_exact + (jnp.log(safe / max_exact) / math.log(REL_MAX_DIST / max_exact)
                         * (REL_BUCKETS - max_exact)).astype(jnp.int32)
    large = jnp.minimum(large, REL_BUCKETS - 1)
    return jnp.where(dist < max_exact, dist, large).astype(jnp.int32)


def _gelu_tanh(x):
    k1 = -2.0 * math.sqrt(2.0 / math.pi)
    return x / (1.0 + jnp.exp(x * (k1 + (k1 * 0.044715) * (x * x))))


def _layer_kernel(relb_ref, sink_ref,
                  x_ref, xnext_ref, preg_ref, win_ref, lng_ref, lnb_ref, wsp_ref, bsx_ref, bucket_ref,
                  wout_ref, postg_ref,
                  o_ref,
                  kz0_ref, kz1_ref, va0_ref, va1_ref, wcat_ref, bias_ref,
                  pend_u_ref, pend_mixed_ref, pend_ga_ref, pend_gb_ref, pend_q_ref, pend_kv_ref,
                  h_ref, win_s_ref, wout_s_ref):
    f32, bf16 = jnp.float32, jnp.bfloat16
    t = pl.program_id(1)
    first_step = (pl.program_id(0) == 0) & (t == 0)

    row = lax.broadcasted_iota(jnp.int32, (BLOCK, BLOCK), 0)
    col = lax.broadcasted_iota(jnp.int32, (BLOCK, BLOCK), 1)
    tril = col <= row
    low_half = col < HEAD_DIM

    @pl.when(first_step)
    def _init():
        half_packed = BLOCK // 2
        low_lanes = lax.broadcasted_iota(jnp.int32, (half_packed, BLOCK), 1) < HEAD_DIM

        def pair_head_columns(chunk):
            words = pltpu.bitcast(chunk, jnp.uint32)
            old = [words[:, s * BLOCK:(s + 1) * BLOCK] for s in range(N_PAIRS)]
            swapped = [pltpu.roll(slab, HEAD_DIM, axis=1) for slab in old]
            new = []
            for g in range(B_GROUP):
                src0, src1 = g // 2, B_GROUP // 2 + g // 2
                if g % 2 == 0:
                    new.append(jnp.where(low_lanes, old[src0], swapped[src1]))
                else:
                    new.append(jnp.where(low_lanes, swapped[src0], old[src1]))
            return pltpu.bitcast(jnp.concatenate(new, axis=1), bf16)

        for r in range(0, D_MODEL, BLOCK):
            rows = slice(r, r + BLOCK)
            win_s_ref[rows, :Q_LO] = win_ref[rows, :Q_LO]
            win_s_ref[rows, Q_LO:K_LO] = pair_head_columns(win_ref[rows, Q_LO:K_LO])
            win_s_ref[rows, K_LO:BZ_LO] = win_ref[rows, K_LO:BZ_LO]
            win_s_ref[rows, BZ_LO:] = pair_head_columns(win_ref[rows, BZ_LO:])
        wout_s_ref[:A_WIDTH, :] = wout_ref[:A_WIDTH, :]
        for g in range(B_GROUP):
            for kv in range(B_KV_HEADS):
                dst = A_WIDTH + (g * B_KV_HEADS + kv) * HEAD_DIM
                src = A_WIDTH + (kv * B_GROUP + g) * HEAD_DIM
                wout_s_ref[dst:dst + HEAD_DIM, :] = wout_ref[src:src + HEAD_DIM, :]
        for p in range(N_PAIRS):
            wcat_ref[p] = jnp.concatenate(
                [jnp.where(tril, wsp_ref[2 * p + i], 0.0).astype(bf16) for i in range(2)], axis=1)
        bucket = bucket_ref[...]
        for h in range(B_HEADS):
            bias_ref[0, h] = jnp.zeros((BLOCK, BLOCK), f32)

        def fill(b, carry):
            hit = bucket == b
            for h in range(B_HEADS):
                bias_ref[0, h] = jnp.where(hit, relb_ref[b, h], bias_ref[0, h])
            return carry
        lax.fori_loop(0, REL_BUCKETS, fill, 0)
        for h in range(B_HEADS):
            bias_ref[1, h] = jnp.where(tril, bias_ref[0, h], NEG_BIG)

    carry_refs = (kz0_ref, kz1_ref, va0_ref, va1_ref)

    @pl.when(t == 0)
    def _reset_carry():
        for ref in carry_refs:
            ref[...] = jnp.zeros((BLOCK, BLOCK), bf16)

    first_flag = (t == 0).astype(jnp.int32)
    prev = tuple(ref[...] for ref in carry_refs)
    ones_cols = jnp.ones((2 * BLOCK, BLOCK), bf16)

    def lane_mask(rows):
        return lax.broadcasted_iota(jnp.int32, (rows, BLOCK), 1) < HEAD_DIM

    def split_rows(a, n):
        return [a[i * BLOCK:(i + 1) * BLOCK] for i in range(n)]

    def pre_norm(xs):
        for i, xb in enumerate(xs):
            ms = jnp.mean(xb * xb, axis=-1, keepdims=True)
            h_ref[i * BLOCK:(i + 1) * BLOCK, :] = (
                xb * lax.rsqrt(ms + NORM_EPS) * preg_ref[...]).astype(bf16)

    def proj(n, lo, width):
        tiles = [jnp.dot(h_ref[0:n * BLOCK, :], win_s_ref[:, c:c + V7X_MXU_COLS],
                         preferred_element_type=f32)
                 for c in range(lo, lo + width, V7X_MXU_COLS)]
        return tiles[0] if len(tiles) == 1 else jnp.concatenate(tiles, axis=1)

    def in_uv_qkv(n):
        uv = proj(n, UV_LO, 2 * A_WIDTH)
        q = (proj(n, Q_LO, B_WIDTH) * (HEAD_DIM ** -0.5)).astype(bf16)
        k_v = proj(n, K_LO, 2 * KV_WIDTH)
        k, vv = k_v[:, :KV_WIDTH].astype(bf16), k_v[:, KV_WIDTH:].astype(bf16)
        low = lane_mask(n * BLOCK)
        zero, one = jnp.zeros((), bf16), jnp.ones((), bf16)
        kv = [split_rows(a, n) for a in (jnp.where(low, k, zero), jnp.where(low, zero, k),
                                         jnp.where(low, vv, one), jnp.where(low, one, vv))]
        sts = [{"q": qi, "kv": tuple(a[i] for a in kv)} for i, qi in enumerate(split_rows(q, n))]
        return sts, uv

    def gelu_and_layer_norm(sts, uv):
        n = len(sts)
        uv = _gelu_tanh(uv)
        u, v = uv[:, :A_WIDTH], uv[:, A_WIDTH:]
        mu = jnp.mean(v, axis=-1, keepdims=True)
        vc = v - mu
        var = jnp.mean(vc * vc, axis=-1, keepdims=True)
        vn = (vc * lax.rsqrt(var + NORM_EPS) * lng_ref[...] + lnb_ref[...]).astype(bf16)
        for st, ui in zip(sts, split_rows(u, n)):
            st["u"] = ui
        return vn

    def spatial_mix(sts, vn):
        n = len(sts)
        zero = jnp.zeros((), bf16)
        mixed = [[] for _ in range(n)]
        for p in range(N_PAIRS):
            cols = []
            for slab in split_rows(vn[:, p * BLOCK:(p + 1) * BLOCK], n):
                cols.append(jnp.concatenate([jnp.where(low_half, slab, zero),
                                             jnp.where(low_half, zero, slab)], axis=0))
            rhs = cols[0] if n == 1 else jnp.concatenate(cols, axis=1)
            out = jnp.dot(wcat_ref[p], rhs, preferred_element_type=f32)
            for i in range(n):
                mixed[i].append(out[:, i * BLOCK:(i + 1) * BLOCK])
        for st, m in zip(sts, mixed):
            st["mixed"] = jnp.concatenate(m, axis=1) + bsx_ref[...]

    def in_gates(sts):
        n = len(sts)
        gate_a = jax.nn.silu(proj(n, AZ_LO, A_WIDTH))
        gate_b = jax.nn.silu(proj(n, BZ_LO, B_WIDTH))
        for st, ga, gb in zip(sts, split_rows(gate_a, n), split_rows(gate_b, n)):
            st["gate_a"], st["gate_b"] = ga, gb

    def scores_and_softmax(st, prev, flag):
        q = st.pop("q")
        q_stack = jnp.concatenate([q[:, g * BLOCK:(g + 1) * BLOCK] for g in range(B_GROUP)], axis=0)
        st["probs"], st["row_max"] = [], []
        for kv in range(B_KV_HEADS):
            keys = jnp.concatenate([prev[kv], st["kv"][kv]], axis=0)
            s = lax.dot_general(q_stack, keys, (((1,), (1,)), ((), ())),
                                preferred_element_type=f32)
            probs, row_max = [], []
            for g in range(B_GROUP):
                head = kv * B_GROUP + g
                sg = s[g * BLOCK:(g + 1) * BLOCK, :]
                logits = jnp.where(tril, sg[:, BLOCK:], sg[:, :BLOCK]) + bias_ref[flag, head]
                sink = sink_ref[head]
                m = jnp.maximum(jnp.max(logits, axis=-1, keepdims=True), sink)
                e = jnp.exp(logits - m)
                e = e.astype(bf16)
                zero = jnp.zeros((), bf16)
                probs.append(jnp.concatenate([jnp.where(tril, zero, e), jnp.where(tril, e, zero)], axis=1))
                row_max.append(m)
            st["probs"].append(jnp.concatenate(probs, axis=0))
            st["row_max"].append(row_max)

    def attend(st, prev):
        pv = []
        for kv in range(B_KV_HEADS):
            vals = jnp.concatenate([jnp.concatenate([prev[2 + kv], st["kv"][2 + kv]], axis=0), ones_cols],
                                   axis=1)
            pv.append(jnp.dot(st["probs"][kv], vals, preferred_element_type=f32))
        attn = []
        for g in range(B_GROUP):
            rows = slice(g * BLOCK, (g + 1) * BLOCK)
            num = jnp.where(low_half, pv[0][rows, :BLOCK], pv[1][rows, :BLOCK])
            row_sum = jnp.where(low_half, pv[0][rows, BLOCK:], pv[1][rows, BLOCK:])
            sink = jnp.where(low_half, sink_ref[g], sink_ref[B_GROUP + g])
            m = jnp.where(low_half, st["row_max"][0][g], st["row_max"][1][g])
            attn.append(num / (row_sum + jnp.exp(sink - m)))
        attn = jnp.concatenate(attn, axis=1)
        st["y_b"] = (attn * st.pop("gate_b")).astype(bf16)

    def out_norm_residual(st, blk):
        r0 = blk * BLOCK
        y_a = ((st.pop("u") * st.pop("mixed")) * st.pop("gate_a")).astype(bf16)
        y = jnp.concatenate([y_a, st.pop("y_b")], axis=1)
        o = jnp.dot(y, wout_s_ref[...], preferred_element_type=f32)
        os_ = jnp.mean(o * o, axis=-1, keepdims=True)
        o_ref[0, r0:r0 + BLOCK, :] = (x_ref[0, r0:r0 + BLOCK, :]
                                      + o * lax.rsqrt(os_ + NORM_EPS) * postg_ref[...])

    def save_pending(st):
        pend_u_ref[...] = st["u"]
        pend_mixed_ref[...] = st["mixed"]
        pend_ga_ref[...] = st["gate_a"]
        pend_gb_ref[...] = st["gate_b"]
        pend_q_ref[...] = st["q"]
        for i in range(2 * B_KV_HEADS):
            pend_kv_ref[i] = st["kv"][i]

    def load_pending():
        return {"u": pend_u_ref[...], "mixed": pend_mixed_ref[...],
                "gate_a": pend_ga_ref[...], "gate_b": pend_gb_ref[...], "q": pend_q_ref[...],
                "kv": tuple(pend_kv_ref[i] for i in range(2 * B_KV_HEADS))}

    @pl.when(first_step)
    def _prologue():
        pre_norm([x_ref[0, 0:BLOCK, :]])
        sts, uv = in_uv_qkv(1)
        spatial_mix(sts, gelu_and_layer_norm(sts, uv))
        in_gates(sts)
        save_pending(sts[0])

    def block_rows(blk):
        if blk < BLOCKS_PER_STEP:
            return x_ref[0, blk * BLOCK:(blk + 1) * BLOCK, :]
        return xnext_ref[0]

    pending = load_pending()
    for blk in range(0, BLOCKS_PER_STEP, 2):
        scores_and_softmax(pending, prev, first_flag if blk == 0 else 0)
        pre_norm([block_rows(blk + 1), block_rows(blk + 2)])
        ahead, uv = in_uv_qkv(2)
        vn = gelu_and_layer_norm(ahead, uv)
        attend(pending, prev)
        out_norm_residual(pending, blk)
        prev = pending["kv"]
        pending = ahead[0]
        scores_and_softmax(pending, prev, 0)
        spatial_mix(ahead, vn)
        in_gates(ahead)
        attend(pending, prev)
        out_norm_residual(pending, blk + 1)
        prev = pending["kv"]
        pending = ahead[1]
    save_pending(pending)

    for ref, val in zip(carry_refs, prev):
        ref[...] = val


def _layer(layer, x, rel_bias, sinks, pre_g, w_in, ln_g, ln_b, w_sp, bs_x, bucket, w_out, post_g):
    bsz, seq, _ = x.shape
    steps = seq // TOKENS_PER_STEP
    const2 = lambda b, t: (0, 0)
    mat_of_layer = lambda b, t: (layer, 0, 0)
    stack_of_layer = lambda b, t: (layer, 0, 0, 0)
    smem = pl.BlockSpec(memory_space=pltpu.SMEM)
    blocks_per_seq = seq // BLOCK
    last_block = bsz * blocks_per_seq - 1

    def next_block_map(b, t):
        n = jnp.minimum(b * blocks_per_seq + (t + 1) * BLOCKS_PER_STEP, last_block)
        return (n // blocks_per_seq, n % blocks_per_seq, 0)

    return pl.pallas_call(
        _layer_kernel,
        grid=(bsz, steps),
        in_specs=[
            smem, smem,
            pl.BlockSpec((1, TOKENS_PER_STEP, D_MODEL), lambda b, t: (b, t, 0)),
            pl.BlockSpec((1, BLOCK, D_MODEL), next_block_map),
            pl.BlockSpec((None, 1, D_MODEL), mat_of_layer),
            pl.BlockSpec((None, D_MODEL, IN_WIDTH), mat_of_layer),
            pl.BlockSpec((None, 1, A_WIDTH), mat_of_layer),
            pl.BlockSpec((None, 1, A_WIDTH), mat_of_layer),
            pl.BlockSpec((None, A_GROUPS, BLOCK, BLOCK), stack_of_layer),
            pl.BlockSpec((None, BLOCK, A_WIDTH), mat_of_layer),
            pl.BlockSpec((BLOCK, BLOCK), const2),
            pl.BlockSpec((None, D_MODEL, D_MODEL), mat_of_layer),
            pl.BlockSpec((None, 1, D_MODEL), mat_of_layer),
        ],
        out_specs=pl.BlockSpec((1, TOKENS_PER_STEP, D_MODEL), lambda b, t: (b, t, 0)),
        out_shape=jax.ShapeDtypeStruct(x.shape, x.dtype),
        scratch_shapes=[
            pltpu.VMEM((BLOCK, BLOCK), jnp.bfloat16),
            pltpu.VMEM((BLOCK, BLOCK), jnp.bfloat16),
            pltpu.VMEM((BLOCK, BLOCK), jnp.bfloat16),
            pltpu.VMEM((BLOCK, BLOCK), jnp.bfloat16),
            pltpu.VMEM((N_PAIRS, BLOCK, 2 * BLOCK), jnp.bfloat16),
            pltpu.VMEM((2, B_HEADS, BLOCK, BLOCK), jnp.float32),
            pltpu.VMEM((BLOCK, A_WIDTH), jnp.float32),
            pltpu.VMEM((BLOCK, A_WIDTH), jnp.float32),
            pltpu.VMEM((BLOCK, A_WIDTH), jnp.float32),
            pltpu.VMEM((BLOCK, B_WIDTH), jnp.float32),
            pltpu.VMEM((BLOCK, B_WIDTH), jnp.bfloat16),
            pltpu.VMEM((2 * B_KV_HEADS, BLOCK, BLOCK), jnp.bfloat16),
            pltpu.VMEM((2 * BLOCK, D_MODEL), jnp.bfloat16),
            pltpu.VMEM((D_MODEL, IN_WIDTH), jnp.bfloat16),
            pltpu.VMEM((D_MODEL, D_MODEL), jnp.bfloat16),
        ],
        compiler_params=pltpu.CompilerParams(
            dimension_semantics=("arbitrary", "arbitrary"),
            allow_input_fusion=[i in (5, 11) for i in range(13)],
            vmem_limit_bytes=_vmem_limit_bytes()),
        name="hybrid_layer",
    )(rel_bias, sinks, x, x, pre_g, w_in, ln_g, ln_b, w_sp, bs_x, bucket, w_out, post_g)


@jax.jit
def kernel(x, pre_norm_g, w_in, ln_v_g, ln_v_b, w_spatial, b_spatial, sinks, rel_bias, w_out, post_norm_g):
    depth = w_in.shape[0]
    w_in_p = w_in.astype(jnp.bfloat16)
    w_out_p = w_out.astype(jnp.bfloat16)
    bs_x = jnp.repeat(jnp.swapaxes(b_spatial, 1, 2), A_GROUP_DIM, axis=2)
    bucket = _combined_bucket_map()
    for layer in range(depth):
        x = _layer(layer, x, rel_bias, sinks[layer], pre_norm_g[:, None], w_in_p, ln_v_g[:, None],
                   ln_v_b[:, None], w_spatial, bs_x, bucket, w_out_p, post_norm_g[:, None])
    return x
```

```python
import math

import jax
import jax.numpy as jnp
from jax import lax
from jax.experimental import pallas as pl
from jax.experimental.pallas import tpu as pltpu

D_MODEL = 1024
A_WIDTH = 512
A_GROUPS = 8
A_GROUP_DIM = A_WIDTH // A_GROUPS
B_HEADS = 8
HEAD_DIM = 64
B_KV_HEADS = 2
B_GROUP = B_HEADS // B_KV_HEADS
B_WIDTH = B_HEADS * HEAD_DIM
KV_WIDTH = B_KV_HEADS * HEAD_DIM
BLOCK = 128
REL_BUCKETS = 32
REL_MAX_DIST = 128
NORM_EPS = 1e-6
IN_WIDTH = 3 * A_WIDTH + 2 * B_WIDTH + 2 * KV_WIDTH
NEG_BIG = -1e30

UV_LO, AZ_LO, Q_LO, K_LO, V_LO, BZ_LO = 0, 2 * A_WIDTH, 3 * A_WIDTH, 3 * A_WIDTH + B_WIDTH, \
    3 * A_WIDTH + B_WIDTH + KV_WIDTH, 3 * A_WIDTH + B_WIDTH + 2 * KV_WIDTH

V7X_LANES = 128
V7X_MXU_COLS = 256
N_PAIRS = A_WIDTH // V7X_LANES
BLOCKS_PER_STEP = 8
TOKENS_PER_STEP = BLOCKS_PER_STEP * BLOCK


def _vmem_limit_bytes():
    f32_bytes, bf16_bytes, buffers = 4, 2, 2
    windows = buffers * (
        2 * TOKENS_PER_STEP * D_MODEL * f32_bytes
        + BLOCK * D_MODEL * f32_bytes
        + D_MODEL * IN_WIDTH * bf16_bytes + D_MODEL * D_MODEL * bf16_bytes
        + (BLOCK * A_WIDTH + BLOCK * BLOCK) * f32_bytes + A_GROUPS * BLOCK * BLOCK * f32_bytes)
    scratch = ((D_MODEL * IN_WIDTH + D_MODEL * D_MODEL) * bf16_bytes
               + 2 * B_HEADS * BLOCK * BLOCK * f32_bytes
               + 4 * BLOCK * A_WIDTH * f32_bytes
               + (N_PAIRS * 2 + B_WIDTH // BLOCK + 4 + 4 + 2 * D_MODEL // BLOCK)
               * BLOCK * BLOCK * bf16_bytes)
    temporaries = 2 * BLOCK * IN_WIDTH * f32_bytes * BLOCKS_PER_STEP // 2
    return windows + scratch + temporaries


def _combined_bucket_map():
    i = jnp.arange(BLOCK)[:, None]
    j = jnp.arange(BLOCK)[None, :]
    dist = jnp.where(j <= i, i - j, i + BLOCK - j)
    max_exact = REL_BUCKETS // 2
    safe = jnp.maximum(dist, 1).astype(jnp.float32)
    large = max_exact + (jnp.log(safe / max_exact) / math.log(REL_MAX_DIST / max_exact)
                         * (REL_BUCKETS - max_exact)).astype(jnp.int32)
    large = jnp.minimum(large, REL_BUCKETS - 1)
    return jnp.where(dist < max_exact, dist, large).astype(jnp.int32)


def _gelu_tanh(x):
    k1 = -2.0 * math.sqrt(2.0 / math.pi)
    return x / (1.0 + jnp.exp(x * (k1 + (k1 * 0.044715) * (x * x))))


def _layer_kernel(relb_ref, sink_ref,
                  x_ref, xnext_ref, preg_ref, win_ref, lng_ref, lnb_ref, wsp_ref, bsx_ref, bucket_ref,
                  wout_ref, postg_ref,
                  o_ref,
                  kz0_ref, kz1_ref, va0_ref, va1_ref, wcat_ref, bias_ref,
                  pend_u_ref, pend_mixed_ref, pend_ga_ref, pend_gb_ref, pend_q_ref, pend_kv_ref,
                  h_ref, win_s_ref, wout_s_ref):
    f32, bf16 = jnp.float32, jnp.bfloat16
    t = pl.program_id(1)
    first_step = (pl.program_id(0) == 0) & (t == 0)

    row = lax.broadcasted_iota(jnp.int32, (BLOCK, BLOCK), 0)
    col = lax.broadcasted_iota(jnp.int32, (BLOCK, BLOCK), 1)
    tril = col <= row
    low_half = col < HEAD_DIM

    @pl.when(first_step)
    def _init():
        half_packed = BLOCK // 2
        low_lanes = lax.broadcasted_iota(jnp.int32, (half_packed, BLOCK), 1) < HEAD_DIM

        def pair_head_columns(chunk):
            words = pltpu.bitcast(chunk, jnp.uint32)
            old = [words[:, s * BLOCK:(s + 1) * BLOCK] for s in range(N_PAIRS)]
            swapped = [pltpu.roll(slab, HEAD_DIM, axis=1) for slab in old]
            new = []
            for g in range(B_GROUP):
                src0, src1 = g // 2, B_GROUP // 2 + g // 2
                if g % 2 == 0:
                    new.append(jnp.where(low_lanes, old[src0], swapped[src1]))
                else:
                    new.append(jnp.where(low_lanes, swapped[src0], old[src1]))
            return pltpu.bitcast(jnp.concatenate(new, axis=1), bf16)

        for r in range(0, D_MODEL, BLOCK):
            rows = slice(r, r + BLOCK)
            win_s_ref[rows, :Q_LO] = win_ref[rows, :Q_LO]
            win_s_ref[rows, Q_LO:K_LO] = pair_head_columns(win_ref[rows, Q_LO:K_LO])
            win_s_ref[rows, K_LO:BZ_LO] = win_ref[rows, K_LO:BZ_LO]
            win_s_ref[rows, BZ_LO:] = pair_head_columns(win_ref[rows, BZ_LO:])
        wout_s_ref[:A_WIDTH, :] = wout_ref[:A_WIDTH, :]
        for g in range(B_GROUP):
            for kv in range(B_KV_HEADS):
                dst = A_WIDTH + (g * B_KV_HEADS + kv) * HEAD_DIM
                src = A_WIDTH + (kv * B_GROUP + g) * HEAD_DIM
                wout_s_ref[dst:dst + HEAD_DIM, :] = wout_ref[src:src + HEAD_DIM, :]
        for p in range(N_PAIRS):
            wcat_ref[p] = jnp.concatenate(
                [jnp.where(tril, wsp_ref[2 * p + i], 0.0).astype(bf16) for i in range(2)], axis=1)
        bucket = bucket_ref[...]
        for h in range(B_HEADS):
            bias_ref[0, h] = jnp.zeros((BLOCK, BLOCK), f32)

        def fill(b, carry):
            hit = bucket == b
            for h in range(B_HEADS):
                bias_ref[0, h] = jnp.where(hit, relb_ref[b, h], bias_ref[0, h])
            return carry
        lax.fori_loop(0, REL_BUCKETS, fill, 0)
        for h in range(B_HEADS):
            bias_ref[1, h] = jnp.where(tril, bias_ref[0, h], NEG_BIG)

    carry_refs = (kz0_ref, kz1_ref, va0_ref, va1_ref)

    @pl.when(t == 0)
    def _reset_carry():
        for ref in carry_refs:
            ref[...] = jnp.zeros((BLOCK, BLOCK), bf16)

    first_flag = (t == 0).astype(jnp.int32)
    prev = tuple(ref[...] for ref in carry_refs)
    ones_cols = jnp.ones((2 * BLOCK, BLOCK), bf16)

    def lane_mask(rows):
        return lax.broadcasted_iota(jnp.int32, (rows, BLOCK), 1) < HEAD_DIM

    def split_rows(a, n):
        return [a[i * BLOCK:(i + 1) * BLOCK] for i in range(n)]

    def pre_norm(xs):
        for i, xb in enumerate(xs):
            ms = jnp.mean(xb * xb, axis=-1, keepdims=True)
            h_ref[i * BLOCK:(i + 1) * BLOCK, :] = (
                xb * lax.rsqrt(ms + NORM_EPS) * preg_ref[...]).astype(bf16)

    def proj(n, lo, width):
        tiles = [jnp.dot(h_ref[0:n * BLOCK, :], win_s_ref[:, c:c + V7X_MXU_COLS],
                         preferred_element_type=f32)
                 for c in range(lo, lo + width, V7X_MXU_COLS)]
        return tiles[0] if len(tiles) == 1 else jnp.concatenate(tiles, axis=1)

    def in_uv_qkv(n):
        uv = proj(n, UV_LO, 2 * A_WIDTH)
        q = (proj(n, Q_LO, B_WIDTH) * (HEAD_DIM ** -0.5)).astype(bf16)
        k_v = proj(n, K_LO, 2 * KV_WIDTH)
        k, vv = k_v[:, :KV_WIDTH].astype(bf16), k_v[:, KV_WIDTH:].astype(bf16)
        low = lane_mask(n * BLOCK)
        zero, one = jnp.zeros((), bf16), jnp.ones((), bf16)
        kv = [split_rows(a, n) for a in (jnp.where(low, k, zero), jnp.where(low, zero, k),
                                         jnp.where(low, vv, one), jnp.where(low, one, vv))]
        sts = [{"q": qi, "kv": tuple(a[i] for a in kv)} for i, qi in enumerate(split_rows(q, n))]
        return sts, uv

    def gelu_and_layer_norm(sts, uv):
        n = len(sts)
        uv = _gelu_tanh(uv)
        u, v = uv[:, :A_WIDTH], uv[:, A_WIDTH:]
        mu = jnp.mean(v, axis=-1, keepdims=True)
        vc = v - mu
        var = jnp.mean(vc * vc, axis=-1, keepdims=True)
        vn = (vc * lax.rsqrt(var + NORM_EPS) * lng_ref[...] + lnb_ref[...]).astype(bf16)
        for st, ui in zip(sts, split_rows(u, n)):
            st["u"] = ui
        return vn

    def spatial_mix(sts, vn):
        n = len(sts)
        zero = jnp.zeros((), bf16)
        mixed = [[] for _ in range(n)]
        for p in range(N_PAIRS):
            cols = []
            for slab in split_rows(vn[:, p * BLOCK:(p + 1) * BLOCK], n):
                cols.append(jnp.concatenate([jnp.where(low_half, slab, zero),
                                             jnp.where(low_half, zero, slab)], axis=0))
            rhs = cols[0] if n == 1 else jnp.concatenate(cols, axis=1)
            out = jnp.dot(wcat_ref[p], rhs, preferred_element_type=f32)
            for i in range(n):
                mixed[i].append(out[:, i * BLOCK:(i + 1) * BLOCK])
        for st, m in zip(sts, mixed):
            st["mixed"] = jnp.concatenate(m, axis=1) + bsx_ref[...]

    def in_gates(sts):
        n = len(sts)
        gate_a = jax.nn.silu(proj(n, AZ_LO, A_WIDTH))
        gate_b = jax.nn.silu(proj(n, BZ_LO, B_WIDTH))
        for st, ga, gb in zip(sts, split_rows(gate_a, n), split_rows(gate_b, n)):
            st["gate_a"], st["gate_b"] = ga, gb

    def scores_and_softmax(st, prev, flag):
        q = st.pop("q")
        q_stack = jnp.concatenate([q[:, g * BLOCK:(g + 1) * BLOCK] for g in range(B_GROUP)], axis=0)
        st["probs"], st["row_max"] = [], []
        for kv in range(B_KV_HEADS):
            keys = jnp.concatenate([prev[kv], st["kv"][kv]], axis=0)
            s = lax.dot_general(q_stack, keys, (((1,), (1,)), ((), ())),
                                preferred_element_type=f32)
            probs, row_max = [], []
            for g in range(B_GROUP):
                head = kv * B_GROUP + g
                sg = s[g * BLOCK:(g + 1) * BLOCK, :]
                logits = jnp.where(tril, sg[:, BLOCK:], sg[:, :BLOCK]) + bias_ref[flag, head]
                sink = sink_ref[head]
                m = jnp.maximum(jnp.max(logits, axis=-1, keepdims=True), sink)
                e = jnp.exp(logits - m)
                e = e.astype(bf16)
                zero = jnp.zeros((), bf16)
                probs.append(jnp.concatenate([jnp.where(tril, zero, e), jnp.where(tril, e, zero)], axis=1))
                row_max.append(m)
            st["probs"].append(jnp.concatenate(probs, axis=0))
            st["row_max"].append(row_max)

    def attend(st, prev):
        pv = []
        for kv in range(B_KV_HEADS):
            vals = jnp.concatenate([jnp.concatenate([prev[2 + kv], st["kv"][2 + kv]], axis=0), ones_cols],
                                   axis=1)
            pv.append(jnp.dot(st["probs"][kv], vals, preferred_element_type=f32))
        attn = []
        for g in range(B_GROUP):
            rows = slice(g * BLOCK, (g + 1) * BLOCK)
            num = jnp.where(low_half, pv[0][rows, :BLOCK], pv[1][rows, :BLOCK])
            row_sum = jnp.where(low_half, pv[0][rows, BLOCK:], pv[1][rows, BLOCK:])
            sink = jnp.where(low_half, sink_ref[g], sink_ref[B_GROUP + g])
            m = jnp.where(low_half, st["row_max"][0][g], st["row_max"][1][g])
            attn.append(num / (row_sum + jnp.exp(sink - m)))
        attn = jnp.concatenate(attn, axis=1)
        st["y_b"] = (attn * st.pop("gate_b")).astype(bf16)

    def out_norm_residual(st, blk):
        r0 = blk * BLOCK
        y_a = ((st.pop("u") * st.pop("mixed")) * st.pop("gate_a")).astype(bf16)
        y = jnp.concatenate([y_a, st.pop("y_b")], axis=1)
        o = jnp.dot(y, wout_s_ref[...], preferred_element_type=f32)
        os_ = jnp.mean(o * o, axis=-1, keepdims=True)
        o_ref[0, r0:r0 + BLOCK, :] = (x_ref[0, r0:r0 + BLOCK, :]
                                      + o * lax.rsqrt(os_ + NORM_EPS) * postg_ref[...])

    def save_pending(st):
        pend_u_ref[...] = st["u"]
        pend_mixed_ref[...] = st["mixed"]
        pend_ga_ref[...] = st["gate_a"]
        pend_gb_ref[...] = st["gate_b"]
        pend_q_ref[...] = st["q"]
        for i in range(2 * B_KV_HEADS):
            pend_kv_ref[i] = st["kv"][i]

    def load_pending():
        return {"u": pend_u_ref[...], "mixed": pend_mixed_ref[...],
                "gate_a": pend_ga_ref[...], "gate_b": pend_gb_ref[...], "q": pend_q_ref[...],
                "kv": tuple(pend_kv_ref[i] for i in range(2 * B_KV_HEADS))}

    @pl.when(first_step)
    def _prologue():
        pre_norm([x_ref[0, 0:BLOCK, :]])
        sts, uv = in_uv_qkv(1)
        spatial_mix(sts, gelu_and_layer_norm(sts, uv))
        in_gates(sts)
        save_pending(sts[0])

    def block_rows(blk):
        if blk < BLOCKS_PER_STEP:
            return x_ref[0, blk * BLOCK:(blk + 1) * BLOCK, :]
        return xnext_ref[0]

    pending = load_pending()
    for blk in range(0, BLOCKS_PER_STEP, 2):
        scores_and_softmax(pending, prev, first_flag if blk == 0 else 0)
        pre_norm([block_rows(blk + 1), block_rows(blk + 2)])
        ahead, uv = in_uv_qkv(2)
        vn = gelu_and_layer_norm(ahead, uv)
        attend(pending, prev)
        out_norm_residual(pending, blk)
        prev = pending["kv"]
        pending = ahead[0]
        scores_and_softmax(pending, prev, 0)
        spatial_mix(ahead, vn)
        in_gates(ahead)
        attend(pending, prev)
        out_norm_residual(pending, blk + 1)
        prev = pending["kv"]
        pending = ahead[1]
    save_pending(pending)

    for ref, val in zip(carry_refs, prev):
        ref[...] = val


def _layer(layer, x, rel_bias, sinks, pre_g, w_in, ln_g, ln_b, w_sp, bs_x, bucket, w_out, post_g):
    bsz, seq, _ = x.shape
    steps = seq // TOKENS_PER_STEP
    const2 = lambda b, t: (0, 0)
    mat_of_layer = lambda b, t: (layer, 0, 0)
    stack_of_layer = lambda b, t: (layer, 0, 0, 0)
    smem = pl.BlockSpec(memory_space=pltpu.SMEM)
    blocks_per_seq = seq // BLOCK
    last_block = bsz * blocks_per_seq - 1

    def next_block_map(b, t):
        n = jnp.minimum(b * blocks_per_seq + (t + 1) * BLOCKS_PER_STEP, last_block)
        return (n // blocks_per_seq, n % blocks_per_seq, 0)

    return pl.pallas_call(
        _layer_kernel,
        grid=(bsz, steps),
        in_specs=[
            smem, smem,
            pl.BlockSpec((1, TOKENS_PER_STEP, D_MODEL), lambda b, t: (b, t, 0)),
            pl.BlockSpec((1, BLOCK, D_MODEL), next_block_map),
            pl.BlockSpec((None, 1, D_MODEL), mat_of_layer),
            pl.BlockSpec((None, D_MODEL, IN_WIDTH), mat_of_layer),
            pl.BlockSpec((None, 1, A_WIDTH), mat_of_layer),
            pl.BlockSpec((None, 1, A_WIDTH), mat_of_layer),
            pl.BlockSpec((None, A_GROUPS, BLOCK, BLOCK), stack_of_layer),
            pl.BlockSpec((None, BLOCK, A_WIDTH), mat_of_layer),
            pl.BlockSpec((BLOCK, BLOCK), const2),
            pl.BlockSpec((None, D_MODEL, D_MODEL), mat_of_layer),
            pl.BlockSpec((None, 1, D_MODEL), mat_of_layer),
        ],
        out_specs=pl.BlockSpec((1, TOKENS_PER_STEP, D_MODEL), lambda b, t: (b, t, 0)),
        out_shape=jax.ShapeDtypeStruct(x.shape, x.dtype),
        scratch_shapes=[
            pltpu.VMEM((BLOCK, BLOCK), jnp.bfloat16),
            pltpu.VMEM((BLOCK, BLOCK), jnp.bfloat16),
            pltpu.VMEM((BLOCK, BLOCK), jnp.bfloat16),
            pltpu.VMEM((BLOCK, BLOCK), jnp.bfloat16),
            pltpu.VMEM((N_PAIRS, BLOCK, 2 * BLOCK), jnp.bfloat16),
            pltpu.VMEM((2, B_HEADS, BLOCK, BLOCK), jnp.float32),
            pltpu.VMEM((BLOCK, A_WIDTH), jnp.float32),
            pltpu.VMEM((BLOCK, A_WIDTH), jnp.float32),
            pltpu.VMEM((BLOCK, A_WIDTH), jnp.float32),
            pltpu.VMEM((BLOCK, B_WIDTH), jnp.float32),
            pltpu.VMEM((BLOCK, B_WIDTH), jnp.bfloat16),
            pltpu.VMEM((2 * B_KV_HEADS, BLOCK, BLOCK), jnp.bfloat16),
            pltpu.VMEM((2 * BLOCK, D_MODEL), jnp.bfloat16),
            pltpu.VMEM((D_MODEL, IN_WIDTH), jnp.bfloat16),
            pltpu.VMEM((D_MODEL, D_MODEL), jnp.bfloat16),
        ],
        compiler_params=pltpu.CompilerParams(
            dimension_semantics=("arbitrary", "arbitrary"),
            vmem_limit_bytes=_vmem_limit_bytes()),
        name="hybrid_layer",
    )(rel_bias, sinks, x, x, pre_g, w_in, ln_g, ln_b, w_sp, bs_x, bucket, w_out, post_g)


@jax.jit
def kernel(x, pre_norm_g, w_in, ln_v_g, ln_v_b, w_spatial, b_spatial, sinks, rel_bias, w_out, post_norm_g):
    depth = w_in.shape[0]
    assert x.ndim == 3 and x.shape[2] == D_MODEL and x.shape[1] % TOKENS_PER_STEP == 0, x.shape
    assert BLOCKS_PER_STEP % 2 == 0
    assert w_in.shape == (depth, D_MODEL, IN_WIDTH) and w_out.shape == (depth, D_MODEL, D_MODEL)
    assert w_spatial.shape == (depth, A_GROUPS, BLOCK, BLOCK) and b_spatial.shape == (depth, A_GROUPS, BLOCK)
    assert sinks.shape == (depth, B_HEADS) and rel_bias.shape == (REL_BUCKETS, B_HEADS)
    w_in_p = w_in.astype(jnp.bfloat16)
    w_out_p = w_out.astype(jnp.bfloat16)
    bs_x = jnp.repeat(jnp.swapaxes(b_spatial, 1, 2), A_GROUP_DIM, axis=2)
    bucket = _combined_bucket_map()
    for layer in range(depth):
        x = _layer(layer, x, rel_bias, sinks[layer], pre_norm_g[:, None], w_in_p, ln_v_g[:, None],
                   ln_v_b[:, None], w_spatial, bs_x, bucket, w_out_p, post_norm_g[:, None])
    return x
```
